```python
import jax, jax.numpy as jnp
from jax import lax
import numpy as np

D_MODEL = 1024
BATCH = 32
SEQ = 256
DEPTH = 2
DEC_BATCH = 2
DEC_SEQ = 4096
PAST_LEN = 256

GRID_W = 64
N_MIXERS = 2
N_NA_LAYERS = (DEPTH + 1) // 2
N_GLA_LAYERS = DEPTH // 2
NA_HEADS = 16
NA_HEAD_DIM = D_MODEL // NA_HEADS
NA_WIN_ROWS = 8
NA_WIN_COLS = 16
GLA_HEADS = 4
GLA_DK = D_MODEL // 2 // GLA_HEADS
GLA_DV = D_MODEL // GLA_HEADS
GLA_GATE_RANK = 16
GLA_GATE_NORM = 16.0
GLA_CHUNK = 64
ROPE_THETA = 10000.0
N_EXPERTS = 32
TOP_K = 4
D_FF = D_MODEL
SWIGLU_LIMIT = 7.0
SWIGLU_ALPHA = 1.702
EPS = 1e-6
NEG_INF = -1e30
BLOCK_Q = 128

kernel_name = 'hybrid_natten_gla_moe_dit_step'


def rms_norm(x, g):
    xf = x.astype(jnp.float32)
    y = xf * lax.rsqrt(jnp.mean(xf * xf, axis=-1, keepdims=True) + EPS)
    return (y * g.astype(jnp.float32)).astype(x.dtype)


def modulation(cond, w_ada, b_ada):
    m = jax.nn.silu(cond) @ w_ada + b_ada
    return jnp.split(m[..., None, :], 6, axis=-1)


def axial_rope(x):
    n = x.shape[1]
    t = jnp.arange(n)
    row = (t // GRID_W).astype(jnp.float32)
    col = (t % GRID_W).astype(jnp.float32)
    half = x.shape[-1] // 2
    n_freq = half // 2
    inv = ROPE_THETA ** (-jnp.arange(n_freq, dtype=jnp.float32) / n_freq)

    def rot(xa, pos):
        ang = pos[:, None] * inv[None, :]
        cos = jnp.cos(ang)[None, :, None, :]
        sin = jnp.sin(ang)[None, :, None, :]
        x1, x2 = xa[..., :n_freq], xa[..., n_freq:]
        return jnp.concatenate([x1 * cos - x2 * sin, x1 * sin + x2 * cos], axis=-1)

    xf = x.astype(jnp.float32)
    return jnp.concatenate([rot(xf[..., :half], row), rot(xf[..., half:], col)], axis=-1).astype(x.dtype)


def split_qkv(qkv):
    b, n, _ = qkv.shape
    qkv = qkv.reshape(b, n, 3, NA_HEADS, NA_HEAD_DIM)
    return qkv[:, :, 0], qkv[:, :, 1], qkv[:, :, 2]


def context_attention(q, k, v):
    b, s, h, hd = q.shape
    nb = s // BLOCK_Q
    qb = q.reshape(b, nb, BLOCK_Q, h, hd).transpose(1, 0, 2, 3, 4)
    scale = hd ** -0.5

    def one_block(qblk):
        sc = jnp.einsum('bqhd,bkhd->bhqk', qblk, k).astype(jnp.float32) * scale
        p = jax.nn.softmax(sc, axis=-1).astype(v.dtype)
        return jnp.einsum('bhqk,bkhd->bqhd', p, v)

    o = lax.map(one_block, qb)
    return o.transpose(1, 0, 2, 3, 4).reshape(b, s, h * hd)


def neighbourhood_attention(q, k, v, ctx_k, ctx_v, rpb):
    b, n, h, hd = q.shape
    rows = n // GRID_W
    kr = min(NA_WIN_ROWS, rows)
    kc = NA_WIN_COLS
    r = np.arange(rows)
    row_idx = np.clip(r - kr // 2, 0, rows - kr)[:, None] + np.arange(kr)[None, :]
    cidx = np.arange(GRID_W)
    col_start = np.clip(cidx - kc // 2, 0, GRID_W - kc)
    col_ok = (cidx[None, :] >= col_start[:, None]) & (cidx[None, :] < col_start[:, None] + kc)
    roff = row_idx - r[:, None] + NA_WIN_ROWS - 1
    coff = np.clip(cidx[None, :] - cidx[:, None], -(kc - 1), kc - 1) + NA_WIN_COLS - 1
    bias = rpb[:, roff[:, None, :, None], coff[None, :, None, :]].astype(jnp.float32)
    bias = jnp.where(col_ok[None, None, :, None, :], bias, NEG_INF)
    qg = q.reshape(b, rows, GRID_W, h, hd)
    kg = k.reshape(b, rows, GRID_W, h, hd)[:, row_idx]
    vg = v.reshape(b, rows, GRID_W, h, hd)[:, row_idx]
    scale = hd ** -0.5
    s_win = jnp.einsum('brchd,brkwhd->bhrckw', qg, kg).astype(jnp.float32) * scale + bias[None]
    s_ctx = jnp.einsum('brchd,bshd->bhrcs', qg, ctx_k).astype(jnp.float32) * scale
    nwin = kr * GRID_W
    s_all = jnp.concatenate([s_win.reshape(b, h, rows, GRID_W, nwin), s_ctx], axis=-1)
    p = jax.nn.softmax(s_all, axis=-1).astype(v.dtype)
    p_win = p[..., :nwin].reshape(b, h, rows, GRID_W, kr, GRID_W)
    p_ctx = p[..., nwin:]
    o = jnp.einsum('bhrckw,brkwhd->brchd', p_win, vg) + jnp.einsum('bhrcs,bshd->brchd', p_ctx, ctx_v)
    return o.reshape(b, n, h * hd)


def gla_scan(q, k, v, g, s0):
    b, n, h, dk = q.shape
    dv = v.shape[-1]
    L = GLA_CHUNK
    nc = n // L
    f32 = jnp.float32

    def chunks(x):
        return x.astype(f32).reshape(b, nc, L, h, x.shape[-1]).transpose(0, 1, 3, 2, 4)

    qc, kc, vc, gc = chunks(q), chunks(k), chunks(v), chunks(g)
    qc = qc * dk ** -0.5
    cum = jnp.cumsum(gc, axis=3)
    cum_last = cum[:, :, :, -1:, :]
    q_dec = qc * jnp.exp(cum)
    k_dec = kc * jnp.exp(-cum)
    causal = np.tril(np.ones((L, L), dtype=bool))
    a = jnp.where(causal, jnp.einsum('bchid,bchjd->bchij', q_dec, k_dec), 0.0)
    o_intra = jnp.einsum('bchij,bchje->bchie', a, vc)
    kv = jnp.einsum('bchjd,bchje->bchde', kc * jnp.exp(cum_last - cum), vc)
    decay = jnp.exp(cum_last[:, :, :, 0, :])

    def step(s, inp):
        d, kv_c = inp
        return d[..., None] * s + kv_c, s

    s_final, s_in = lax.scan(step, s0.astype(f32),
                             (decay.transpose(1, 0, 2, 3), kv.transpose(1, 0, 2, 3, 4)))
    s_in = s_in.transpose(1, 0, 2, 3, 4)
    o = o_intra + jnp.einsum('bchid,bchde->bchie', q_dec, s_in)
    return o.transpose(0, 1, 3, 2, 4).reshape(b, n, h, dv), s_final


def gla_mixer(h_in, w_in, w_g1_f, w_g2_f, b_g_f, w_g1_b, w_g2_b, b_g_b, norm_g, w_out, s_f0, s_b0, rotary):
    b, n, _ = h_in.shape
    hk = GLA_HEADS * GLA_DK
    hv = GLA_HEADS * GLA_DV
    proj = h_in @ w_in
    q, k, v, r = jnp.split(proj, [hk, 2 * hk, 2 * hk + hv], axis=-1)
    q = q.reshape(b, n, GLA_HEADS, GLA_DK)
    k = k.reshape(b, n, GLA_HEADS, GLA_DK)
    v = v.reshape(b, n, GLA_HEADS, GLA_DV)
    if rotary:
        q = axial_rope(q)
        k = axial_rope(k)
    g_f = jax.nn.log_sigmoid(((h_in @ w_g1_f) @ w_g2_f + b_g_f).astype(jnp.float32)) / GLA_GATE_NORM
    g_b = jax.nn.log_sigmoid(((h_in @ w_g1_b) @ w_g2_b + b_g_b).astype(jnp.float32)) / GLA_GATE_NORM
    g_f = g_f.reshape(b, n, GLA_HEADS, GLA_DK)
    g_b = g_b.reshape(b, n, GLA_HEADS, GLA_DK)
    o_f, s_f = gla_scan(q, k, v, g_f, s_f0)
    o_b, s_b = gla_scan(q[:, ::-1], k[:, ::-1], v[:, ::-1], g_b[:, ::-1], s_b0)
    o = rms_norm(o_f + o_b[:, ::-1], norm_g).astype(h_in.dtype).reshape(b, n, hv)
    y = (o * jax.nn.silu(r)) @ w_out
    return y, s_f.astype(h_in.dtype), s_b.astype(h_in.dtype)


def moe_ffn(h_in, w_router, b_router, w_up, b_up, w_down, b_down):
    b, n, d = h_in.shape
    xt = h_in.reshape(b * n, d)
    logits = (xt @ w_router + b_router).astype(jnp.float32)
    top_val, top_idx = lax.top_k(logits, TOP_K)
    top_w = jax.nn.softmax(top_val, axis=-1)
    combine = jnp.sum(jax.nn.one_hot(top_idx, N_EXPERTS, dtype=jnp.float32) * top_w[..., None], axis=1)
    combine = combine.astype(h_in.dtype)

    def expert(acc, inp):
        wu, bu, wd, bd, cw = inp
        u = xt @ wu + bu
        gate = jnp.minimum(u[:, :D_FF], SWIGLU_LIMIT)
        lin = jnp.clip(u[:, D_FF:], -SWIGLU_LIMIT, SWIGLU_LIMIT)
        y = (gate * jax.nn.sigmoid(SWIGLU_ALPHA * gate) * (lin + 1.0)) @ wd + bd
        return acc + cw[:, None] * y, None

    out, _ = lax.scan(expert, jnp.zeros_like(xt), (w_up, b_up, w_down, b_down, combine.T))
    return out.reshape(b, n, d)


def setup_inputs(seed: int = 0) -> dict:
    key = jax.random.key(seed)
    ks = iter(jax.random.split(key, 40))
    f32 = jnp.float32
    D = D_MODEL
    hk = GLA_HEADS * GLA_DK
    hv = GLA_HEADS * GLA_DV
    R = GLA_GATE_RANK

    def nrm(shape, scale=1.0):
        return jax.random.normal(next(ks), shape, f32) * scale

    def gain(shape):
        return 1.0 + nrm(shape, 0.05)

    return {
        'x_prompt': nrm((BATCH, SEQ, D)),
        'x_sample': nrm((DEC_BATCH, DEC_SEQ, D)),
        'cache_k': nrm((DEC_BATCH, N_NA_LAYERS, PAST_LEN, NA_HEADS, NA_HEAD_DIM)),
        'cache_v': nrm((DEC_BATCH, N_NA_LAYERS, PAST_LEN, NA_HEADS, NA_HEAD_DIM)),
        'state_fwd': nrm((DEC_BATCH, N_GLA_LAYERS, GLA_HEADS, GLA_DK, GLA_DV)),
        'state_bwd': nrm((DEC_BATCH, N_GLA_LAYERS, GLA_HEADS, GLA_DK, GLA_DV)),
        'c': nrm((DEC_BATCH, D)),
        'c_ctx': nrm((D,)),
        'w_ada': nrm((DEPTH, D, 6 * D), 0.3 * D ** -0.5),
        'b_ada': nrm((DEPTH, 6 * D), 0.02),
        'g_pre_mix': gain((DEPTH, D)),
        'g_post_mix': gain((DEPTH, D)),
        'g_pre_ffn': gain((DEPTH, D)),
        'g_post_ffn': gain((DEPTH, D)),
        'na_w_qkv': nrm((N_NA_LAYERS, D, 3 * D), D ** -0.5),
        'na_rpb': nrm((N_NA_LAYERS, NA_HEADS, 2 * NA_WIN_ROWS - 1, 2 * NA_WIN_COLS - 1), 0.1),
        'na_w_out': nrm((N_NA_LAYERS, D, D), D ** -0.5),
        'gla_w_in': nrm((N_GLA_LAYERS, D, 2 * hk + 2 * hv), D ** -0.5),
        'gla_w_g1_fwd': nrm((N_GLA_LAYERS, D, R), D ** -0.5),
        'gla_w_g2_fwd': nrm((N_GLA_LAYERS, R, hk), R ** -0.5),
        'gla_b_g_fwd': nrm((N_GLA_LAYERS, hk), 0.1),
        'gla_w_g1_bwd': nrm((N_GLA_LAYERS, D, R), D ** -0.5),
        'gla_w_g2_bwd': nrm((N_GLA_LAYERS, R, hk), R ** -0.5),
        'gla_b_g_bwd': nrm((N_GLA_LAYERS, hk), 0.1),
        'gla_norm_g': gain((N_GLA_LAYERS, GLA_DV)),
        'gla_w_out': nrm((N_GLA_LAYERS, hv, D), hv ** -0.5),
        'moe_w_router': nrm((DEPTH, D, N_EXPERTS), D ** -0.5),
        'moe_b_router': nrm((DEPTH, N_EXPERTS), 0.01),
        'moe_w_up': nrm((DEPTH, N_EXPERTS, D, 2 * D_FF), D ** -0.5),
        'moe_b_up': nrm((DEPTH, N_EXPERTS, 2 * D_FF), 0.02),
        'moe_w_down': nrm((DEPTH, N_EXPERTS, D_FF, D), D_FF ** -0.5),
        'moe_b_down': nrm((DEPTH, N_EXPERTS, D), 0.02),
    }


def reference(x_prompt, x_sample, cache_k, cache_v, state_fwd, state_bwd, c, c_ctx,
              w_ada, b_ada, g_pre_mix, g_post_mix, g_pre_ffn, g_post_ffn,
              na_w_qkv, na_rpb, na_w_out,
              gla_w_in, gla_w_g1_fwd, gla_w_g2_fwd, gla_b_g_fwd, gla_w_g1_bwd, gla_w_g2_bwd, gla_b_g_bwd,
              gla_norm_g, gla_w_out,
              moe_w_router, moe_b_router, moe_w_up, moe_b_up, moe_w_down, moe_b_down):
    xp, xs = x_prompt, x_sample
    bp = xp.shape[0]
    new_k, new_v, new_sf, new_sb = [], [], [], []
    for i in range(DEPTH):
        j = i // N_MIXERS
        sh1_p, sc1_p, ga1_p, sh2_p, sc2_p, ga2_p = modulation(c_ctx, w_ada[i], b_ada[i])
        sh1_s, sc1_s, ga1_s, sh2_s, sc2_s, ga2_s = modulation(c, w_ada[i], b_ada[i])
        hp = rms_norm(xp, g_pre_mix[i]) * (1.0 + sc1_p) + sh1_p
        hs = rms_norm(xs, g_pre_mix[i]) * (1.0 + sc1_s) + sh1_s
        if i % N_MIXERS == 0:
            qp, kp, vp = split_qkv(hp @ na_w_qkv[j])
            qs, ks_, vs = split_qkv(hs @ na_w_qkv[j])
            yp = context_attention(qp, kp, vp) @ na_w_out[j]
            ys = neighbourhood_attention(qs, ks_, vs, cache_k[:, j], cache_v[:, j], na_rpb[j]) @ na_w_out[j]
            new_k.append(kp)
            new_v.append(vp)
        else:
            zeros = jnp.zeros((bp, GLA_HEADS, GLA_DK, GLA_DV), jnp.float32)
            yp, sfp, sbp = gla_mixer(hp, gla_w_in[j], gla_w_g1_fwd[j], gla_w_g2_fwd[j], gla_b_g_fwd[j],
                                     gla_w_g1_bwd[j], gla_w_g2_bwd[j], gla_b_g_bwd[j], gla_norm_g[j],
                                     gla_w_out[j], zeros, zeros, False)
            ys, _, _ = gla_mixer(hs, gla_w_in[j], gla_w_g1_fwd[j], gla_w_g2_fwd[j], gla_b_g_fwd[j],
                                 gla_w_g1_bwd[j], gla_w_g2_bwd[j], gla_b_g_bwd[j], gla_norm_g[j],
                                 gla_w_out[j], state_fwd[:, j], state_bwd[:, j], True)
            new_sf.append(sfp)
            new_sb.append(sbp)
        xp = xp + ga1_p * rms_norm(yp, g_post_mix[i])
        xs = xs + ga1_s * rms_norm(ys, g_post_mix[i])
        hp = rms_norm(xp, g_pre_ffn[i]) * (1.0 + sc2_p) + sh2_p
        hs = rms_norm(xs, g_pre_ffn[i]) * (1.0 + sc2_s) + sh2_s
        fp = moe_ffn(hp, moe_w_router[i], moe_b_router[i], moe_w_up[i], moe_b_up[i], moe_w_down[i], moe_b_down[i])
        fs = moe_ffn(hs, moe_w_router[i], moe_b_router[i], moe_w_up[i], moe_b_up[i], moe_w_down[i], moe_b_down[i])
        xp = xp + ga2_p * rms_norm(fp, g_post_ffn[i])
        xs = xs + ga2_s * rms_norm(fs, g_post_ffn[i])
    new_cache_k = jnp.stack(new_k, axis=1)
    new_cache_v = jnp.stack(new_v, axis=1)
    new_state_fwd = jnp.stack(new_sf, axis=1)
    new_state_bwd = jnp.stack(new_sb, axis=1)
    return (xp, xs, new_cache_k, new_cache_v, new_state_fwd, new_state_bwd)
```

```python
import functools

import numpy as np
import jax
import jax.numpy as jnp
from jax import lax
from jax.experimental import pallas as pl
from jax.experimental.pallas import tpu as pltpu

F32 = jnp.float32
BF16 = jnp.bfloat16
I32 = jnp.int32
HIGHEST = lax.Precision.HIGHEST

D = 1024
N_CTX_SEQ = 32
CTX_LEN = 256
N_LAT_SEQ = 2
LAT_LEN = 4096
T_CTX = N_CTX_SEQ * CTX_LEN
T_LAT = N_LAT_SEQ * LAT_LEN
T_ALL = T_CTX + T_LAT
GRID_W = 64
GRID_ROWS = LAT_LEN // GRID_W
NA_HEADS = 16
NA_HD = 64
NA_WIN_ROWS = 8
NA_WIN_COLS = 16
GLA_HEADS = 4
GLA_DK = 128
GLA_DV = 256
GLA_RANK = 16
GLA_GATE_NORM = 16.0
GLA_CHUNK = 64
ROPE_THETA = 10000.0
N_EXPERTS = 32
TOP_K = 4
D_FF = 1024
SWIGLU_LIMIT = 7.0
SWIGLU_ALPHA = 1.702
EPS = 1e-6
NEG_INF = -1e30

LANES = 128
TM = 512
GLA_BLOCK = 256
MOE_TM = 512
MOE_ROWS = T_ALL * TOP_K
MOE_TILES = MOE_ROWS // MOE_TM
MOE_VISITS = MOE_TILES + N_EXPERTS - 1
TD = 512
TC = 256
VMEM_LIMIT = 60 * 1024 * 1024

_NT = (((1,), (1,)), ((), ()))
_TN = (((0,), (0,)), ((), ()))


def _cparams(n_axes, vmem=None):
    return pltpu.CompilerParams(
        dimension_semantics=("arbitrary",) * n_axes,
        vmem_limit_bytes=VMEM_LIMIT if vmem is None else vmem)


def _mod_idx(i, tm):
    return jnp.maximum((i * tm) // LAT_LEN - 1, 0)


def _rms(x, g):
    return x * lax.rsqrt(jnp.mean(x * x, axis=-1, keepdims=True) + EPS) * g


def _norm_mod(x, g, m_ref, shift_row):
    sh = m_ref[0, shift_row:shift_row + 1, :]
    sc = m_ref[0, shift_row + 1:shift_row + 2, :]
    return _rms(x, g) * (1.0 + sc) + sh


def _mod_kernel(c_ref, w_ref, b_ref, o_ref):
    c = c_ref[...]
    s = c * jax.nn.sigmoid(c)
    o_ref[0] = jnp.dot(s, w_ref[0], precision=HIGHEST, preferred_element_type=F32) + b_ref[0]


def _modulation(cond, w_ada, b_ada):
    depth = w_ada.shape[0]
    out = pl.pallas_call(
        _mod_kernel,
        out_shape=jax.ShapeDtypeStruct((depth, 8, 6 * D), F32),
        grid=(depth, 6),
        in_specs=[pl.BlockSpec((8, D), lambda l, j: (0, 0)),
                  pl.BlockSpec((1, D, D), lambda l, j: (l, 0, j)),
                  pl.BlockSpec((1, 1, D), lambda l, j: (l, 0, j))],
        out_specs=pl.BlockSpec((1, 8, D), lambda l, j: (l, 0, j)),
        compiler_params=_cparams(2),
        name="adaln_modulation",
    )(cond, w_ada, b_ada.reshape(depth, 1, 6 * D))
    return out.reshape(depth, 8, 6, D)


def _nmm_kernel(x_ref, g_ref, m_ref, w_ref, o_ref, wb_ref, *, shift_row):
    @pl.when(pl.program_id(1) == 0)
    def _():
        wb_ref[...] = w_ref[...].astype(BF16)

    h = _norm_mod(x_ref[...], g_ref[...], m_ref, shift_row)
    o_ref[...] = jnp.dot(h.astype(BF16), wb_ref[...], preferred_element_type=F32)


def _norm_mod_matmul(x, g, mod, w, shift_row, tn):
    t, n = x.shape[0], w.shape[1]
    return pl.pallas_call(
        functools.partial(_nmm_kernel, shift_row=shift_row),
        out_shape=jax.ShapeDtypeStruct((t, n), F32),
        grid=(n // tn, t // TM),
        in_specs=[pl.BlockSpec((TM, D), lambda j, i: (i, 0)),
                  pl.BlockSpec((1, D), lambda j, i: (0, 0)),
                  pl.BlockSpec((1, 6, D), lambda j, i: (_mod_idx(i, TM), 0, 0)),
                  pl.BlockSpec((D, tn), lambda j, i: (0, j))],
        out_specs=pl.BlockSpec((TM, tn), lambda j, i: (i, j)),
        scratch_shapes=[pltpu.VMEM((D, tn), BF16)],
        compiler_params=_cparams(2),
        name="norm_mod_matmul",
    )(x, g.reshape(1, D), mod, w)


def _ctx_attn_kernel(qkv_ref, o_ref):
    lane = lax.broadcasted_iota(I32, (1, LANES), 1)
    scale = NA_HD ** -0.5
    for hp in range(NA_HEADS // 2):
        q = qkv_ref[0, :, hp * LANES:(hp + 1) * LANES] * scale
        k = qkv_ref[0, :, D + hp * LANES:D + (hp + 1) * LANES].astype(BF16)
        v = qkv_ref[0, :, 2 * D + hp * LANES:2 * D + (hp + 1) * LANES]
        acc = jnp.zeros((CTX_LEN, LANES), F32)
        for half in range(2):
            msk = (lane < NA_HD) if half == 0 else (lane >= NA_HD)
            qm = jnp.where(msk, q, 0.0).astype(BF16)
            s = lax.dot_general(qm, k, _NT, preferred_element_type=F32)
            p = jnp.exp(s - jnp.max(s, axis=-1, keepdims=True))
            l = jnp.sum(p, axis=-1, keepdims=True)
            vm = jnp.where(msk, v, 0.0).astype(BF16)
            acc = acc + jnp.dot(p.astype(BF16), vm, preferred_element_type=F32) / l
        o_ref[0, :, hp * LANES:(hp + 1) * LANES] = acc


def _ctx_attention(qkv):
    out = pl.pallas_call(
        _ctx_attn_kernel,
        out_shape=jax.ShapeDtypeStruct((N_CTX_SEQ, CTX_LEN, D), F32),
        grid=(N_CTX_SEQ,),
        in_specs=[pl.BlockSpec((1, CTX_LEN, 3 * D), lambda b: (b, 0, 0))],
        out_specs=pl.BlockSpec((1, CTX_LEN, D), lambda b: (b, 0, 0)),
        compiler_params=_cparams(1),
        name="context_attention",
    )(qkv.reshape(T_ALL // CTX_LEN, CTX_LEN, 3 * D))
    return out.reshape(T_CTX, D)


def _na_kernel(q_ref, k_ref, v_ref, ck_ref, cv_ref, bias_ref, o_ref):
    lane = lax.broadcasted_iota(I32, (1, LANES), 1)
    masks = ((lane < NA_HD), (lane >= NA_HD))
    scale = NA_HD ** -0.5
    ck = ck_ref[0].astype(BF16)
    cv = cv_ref[0]
    cvm = [jnp.where(m, cv, 0.0).astype(BF16) for m in masks]
    nwin = NA_WIN_ROWS * GRID_W

    def body(r, carry):
        r0 = jnp.clip(r - NA_WIN_ROWS // 2, 0, GRID_ROWS - NA_WIN_ROWS)
        base = r0 - r + NA_WIN_ROWS - 1
        qs = pl.ds(pl.multiple_of(r * GRID_W, GRID_W), GRID_W)
        ws = pl.ds(pl.multiple_of(r0 * GRID_W, GRID_W), nwin)
        q = q_ref[0, qs, :] * scale
        kw = k_ref[0, ws, :].astype(BF16)
        vw = v_ref[0, ws, :]
        acc = jnp.zeros((GRID_W, LANES), F32)
        for half in range(2):
            qm = jnp.where(masks[half], q, 0.0).astype(BF16)
            sw = lax.dot_general(qm, kw, _NT, preferred_element_type=F32) + bias_ref[half, pl.ds(base, 1)][0]
            sc = lax.dot_general(qm, ck, _NT, preferred_element_type=F32)
            m = jnp.maximum(jnp.max(sw, axis=-1, keepdims=True), jnp.max(sc, axis=-1, keepdims=True))
            pw = jnp.exp(sw - m)
            pc = jnp.exp(sc - m)
            l = jnp.sum(pw, axis=-1, keepdims=True) + jnp.sum(pc, axis=-1, keepdims=True)
            vm = jnp.where(masks[half], vw, 0.0).astype(BF16)
            o = (jnp.dot(pw.astype(BF16), vm, preferred_element_type=F32)
                 + jnp.dot(pc.astype(BF16), cvm[half], preferred_element_type=F32))
            acc = acc + o / l
        o_ref[0, qs, :] = acc
        return carry

    lax.fori_loop(0, GRID_ROWS, body, 0)


def _na_bias_table(rpb):
    cidx = np.arange(GRID_W)
    col_start = np.clip(cidx - NA_WIN_COLS // 2, 0, GRID_W - NA_WIN_COLS)
    col_ok = (cidx[None, :] >= col_start[:, None]) & (cidx[None, :] < col_start[:, None] + NA_WIN_COLS)
    coff = np.clip(cidx[None, :] - cidx[:, None], -(NA_WIN_COLS - 1), NA_WIN_COLS - 1) + NA_WIN_COLS - 1
    roff = np.arange(NA_WIN_ROWS)[:, None] + np.arange(NA_WIN_ROWS)[None, :]
    b = rpb[:, roff[:, None, :, None], coff[None, :, None, :]].astype(F32)
    b = jnp.where(col_ok[None, None, :, None, :], b, NEG_INF)
    return b.reshape(NA_HEADS, NA_WIN_ROWS, GRID_W, NA_WIN_ROWS * GRID_W)


def _na_attention(qkv, cache_k, cache_v, rpb):
    off = T_CTX // LAT_LEN
    qkv3 = qkv.reshape(T_ALL // LAT_LEN, LAT_LEN, 3 * D)
    ck = cache_k.reshape(N_LAT_SEQ, CTX_LEN, D)
    cv = cache_v.reshape(N_LAT_SEQ, CTX_LEN, D)
    bias = _na_bias_table(rpb)
    nh = D // LANES
    out = pl.pallas_call(
        _na_kernel,
        out_shape=jax.ShapeDtypeStruct((N_LAT_SEQ, LAT_LEN, D), F32),
        grid=(N_LAT_SEQ, nh),
        in_specs=[pl.BlockSpec((1, LAT_LEN, LANES), lambda b, h: (b + off, 0, h)),
                  pl.BlockSpec((1, LAT_LEN, LANES), lambda b, h: (b + off, 0, nh + h)),
                  pl.BlockSpec((1, LAT_LEN, LANES), lambda b, h: (b + off, 0, 2 * nh + h)),
                  pl.BlockSpec((1, CTX_LEN, LANES), lambda b, h: (b, 0, h)),
                  pl.BlockSpec((1, CTX_LEN, LANES), lambda b, h: (b, 0, h)),
                  pl.BlockSpec((2, NA_WIN_ROWS, GRID_W, NA_WIN_ROWS * GRID_W), lambda b, h: (h, 0, 0, 0))],
        out_specs=pl.BlockSpec((1, LAT_LEN, LANES), lambda b, h: (b, 0, h)),
        compiler_params=_cparams(2),
        name="neighbourhood_attention",
    )(qkv3, qkv3, qkv3, ck, cv, bias)
    return out.reshape(T_LAT, D)


CTX_TILES = T_CTX // TM


def _ctx_part(i):
    return (jnp.minimum(i, CTX_TILES - 1), 0)


def _lat_part(i):
    return (jnp.maximum(i - CTX_TILES, 0), 0)


def _proj_res_kernel(ac_ref, al_ref, x_ref, m_ref, g_ref, w_ref, o_ref, wb_ref, *, gate_row):
    i = pl.program_id(0)

    @pl.when(i == 0)
    def _():
        wb_ref[...] = w_ref[...].astype(BF16)

    a = jnp.where(i < CTX_TILES, ac_ref[...], al_ref[...])
    y = jnp.dot(a.astype(BF16), wb_ref[...], preferred_element_type=F32)
    o_ref[...] = x_ref[...] + m_ref[0, gate_row:gate_row + 1, :] * _rms(y, g_ref[...])


def _proj_residual(a_ctx, a_lat, x, mod, g_post, w, gate_row):
    return pl.pallas_call(
        functools.partial(_proj_res_kernel, gate_row=gate_row),
        out_shape=jax.ShapeDtypeStruct((T_ALL, D), F32),
        grid=(T_ALL // TM,),
        in_specs=[pl.BlockSpec((TM, D), _ctx_part),
                  pl.BlockSpec((TM, D), _lat_part),
                  pl.BlockSpec((TM, D), lambda i: (i, 0)),
                  pl.BlockSpec((1, 6, D), lambda i: (_mod_idx(i, TM), 0, 0)),
                  pl.BlockSpec((1, D), lambda i: (0, 0)),
                  pl.BlockSpec((D, D), lambda i: (0, 0))],
        out_specs=pl.BlockSpec((TM, D), lambda i: (i, 0)),
        scratch_shapes=[pltpu.VMEM((D, D), BF16)],
        compiler_params=_cparams(1),
        name="proj_residual",
    )(a_ctx, a_lat, x, mod, g_post.reshape(1, D), w)


def _gla_gate_kernel(x_ref, g_ref, m_ref, w1_ref, w2_ref, b_ref, o_ref):
    h = _norm_mod(x_ref[...], g_ref[...], m_ref, 0)
    z = jnp.dot(h.astype(BF16), w1_ref[...].astype(BF16), preferred_element_type=F32)
    y = jnp.dot(z, w2_ref[...], precision=HIGHEST, preferred_element_type=F32) + b_ref[...]
    o_ref[...] = (jnp.minimum(y, 0.0) - jnp.log(1.0 + jnp.exp(-jnp.abs(y)))) * (1.0 / GLA_GATE_NORM)


def _gla_gates(x, g, mod, w1f, w2f, bf, w1b, w2b, bb):
    hk = GLA_HEADS * GLA_DK
    w1 = jnp.zeros((D, LANES), F32).at[:, :GLA_RANK].set(w1f).at[:, GLA_RANK:2 * GLA_RANK].set(w1b)
    w2 = jnp.zeros((LANES, 2 * hk), F32).at[:GLA_RANK, :hk].set(w2f).at[GLA_RANK:2 * GLA_RANK, hk:].set(w2b)
    b = jnp.concatenate([bf, bb]).reshape(1, 2 * hk)
    return pl.pallas_call(
        _gla_gate_kernel,
        out_shape=jax.ShapeDtypeStruct((T_ALL, 2 * hk), F32),
        grid=(T_ALL // TM,),
        in_specs=[pl.BlockSpec((TM, D), lambda i: (i, 0)),
                  pl.BlockSpec((1, D), lambda i: (0, 0)),
                  pl.BlockSpec((1, 6, D), lambda i: (_mod_idx(i, TM), 0, 0)),
                  pl.BlockSpec((D, LANES), lambda i: (0, 0)),
                  pl.BlockSpec((LANES, 2 * hk), lambda i: (0, 0)),
                  pl.BlockSpec((1, 2 * hk), lambda i: (0, 0))],
        out_specs=pl.BlockSpec((TM, 2 * hk), lambda i: (i, 0)),
        compiler_params=_cparams(1),
        name="gla_gates",
    )(x, g.reshape(1, D), mod, w1, w2, b)


def _rope(x, cos, sin_signed):
    lane = lax.broadcasted_iota(I32, (1, LANES), 1)
    partner = jnp.where((lane % 64) < 32, pltpu.roll(x, LANES - 32, 1), pltpu.roll(x, 32, 1))
    return x * cos + partner * sin_signed


def _gla_chunk(q, k, v, g, st, tri, forward):
    L = GLA_CHUNK
    cum = jnp.dot(tri.astype(F32), g, precision=HIGHEST, preferred_element_type=F32)
    cl = cum[L - 1:L, :] if forward else cum[0:1, :]
    q_dec = (q * (GLA_DK ** -0.5) * jnp.exp(cum)).astype(BF16)
    k_dec = (k * jnp.exp(-cum)).astype(BF16)
    k_rem = (k * jnp.exp(cl - cum)).astype(BF16)
    vb = v.astype(BF16)
    a = lax.dot_general(q_dec, k_dec, _NT, preferred_element_type=F32)
    a = jnp.where(tri, a, 0.0).astype(BF16)
    o = (jnp.dot(a, vb, preferred_element_type=F32)
         + lax.dot_general(q_dec, st.astype(BF16), _NT, preferred_element_type=F32))
    kv_t = lax.dot_general(vb, k_rem, _TN, preferred_element_type=F32)
    return o, jnp.exp(cl) * st + kv_t


def _gla_kernel(*refs, rotary, has_init, out_state, nblk):
    refs = list(refs)
    qf, kf, vf, gf, qb, kb, vb, gb = refs[:8]
    del refs[:8]
    if rotary:
        cosf, sinf, cosb, sinb = refs[:4]
        del refs[:4]
    if has_init:
        sf0, sb0 = refs[:2]
        del refs[:2]
    of, ob = refs[:2]
    del refs[:2]
    if out_state:
        sfo, sbo = refs[:2]
        del refs[:2]
    st_f, st_b = refs

    j = pl.program_id(2)
    L = GLA_CHUNK

    @pl.when(j == 0)
    def _():
        if has_init:
            st_f[...] = sf0[0, 0].T
            st_b[...] = sb0[0, 0].T
        else:
            st_f[...] = jnp.zeros_like(st_f)
            st_b[...] = jnp.zeros_like(st_b)

    row = lax.broadcasted_iota(I32, (L, L), 0)
    col = lax.broadcasted_iota(I32, (L, L), 1)
    tri_f = col <= row
    tri_b = col >= row

    nchunk = GLA_BLOCK // L
    s = st_f[...]
    for c in range(nchunk):
        sl = slice(c * L, (c + 1) * L)
        q, k = qf[0, sl, :], kf[0, sl, :]
        if rotary:
            q = _rope(q, cosf[sl, :], sinf[sl, :])
            k = _rope(k, cosf[sl, :], sinf[sl, :])
        o, s = _gla_chunk(q, k, vf[0, sl, :], gf[0, sl, :], s, tri_f, True)
        of[0, sl, :] = o
    st_f[...] = s

    s = st_b[...]
    for c in reversed(range(nchunk)):
        sl = slice(c * L, (c + 1) * L)
        q, k = qb[0, sl, :], kb[0, sl, :]
        if rotary:
            q = _rope(q, cosb[sl, :], sinb[sl, :])
            k = _rope(k, cosb[sl, :], sinb[sl, :])
        o, s = _gla_chunk(q, k, vb[0, sl, :], gb[0, sl, :], s, tri_b, False)
        ob[0, sl, :] = o
    st_b[...] = s

    if out_state:
        @pl.when(j == nblk - 1)
        def _():
            sfo[0, 0] = st_f[...].T
            sbo[0, 0] = st_b[...].T


def _rope_tables(n):
    t = np.arange(n)
    n_freq = GLA_DK // 4
    inv = ROPE_THETA ** (-np.arange(n_freq, dtype=np.float64) / n_freq)
    ang_r = (t // GRID_W).astype(np.float64)[:, None] * inv[None, :]
    ang_c = (t % GRID_W).astype(np.float64)[:, None] * inv[None, :]
    cos = np.concatenate([np.cos(ang_r), np.cos(ang_r), np.cos(ang_c), np.cos(ang_c)], axis=1)
    sin = np.concatenate([-np.sin(ang_r), np.sin(ang_r), -np.sin(ang_c), np.sin(ang_c)], axis=1)
    return jnp.asarray(cos, F32), jnp.asarray(sin, F32)


def _gla_scan(proj, gates, *, seq_len, n_seq, seq_off, s_f0=None, s_b0=None, rotary=False, out_state=False):
    nseq_all = T_ALL // seq_len
    nblk = seq_len // GLA_BLOCK
    nh = GLA_HEADS
    proj3 = proj.reshape(nseq_all, seq_len, 3 * D)
    g3 = gates.reshape(nseq_all, seq_len, 2 * nh * GLA_DK)
    has_init = s_f0 is not None

    def fwd(c0):
        return lambda b, h, j: (b + seq_off, j, c0 + h)

    def bwd(c0):
        return lambda b, h, j: (b + seq_off, nblk - 1 - j, c0 + h)

    qk = (1, GLA_BLOCK, GLA_DK)
    vv = (1, GLA_BLOCK, GLA_DV)
    v_c0 = 2 * nh * GLA_DK // GLA_DV
    in_specs = [pl.BlockSpec(qk, fwd(0)), pl.BlockSpec(qk, fwd(nh)), pl.BlockSpec(vv, fwd(v_c0)), pl.BlockSpec(qk, fwd(0)),
                pl.BlockSpec(qk, bwd(0)), pl.BlockSpec(qk, bwd(nh)), pl.BlockSpec(vv, bwd(v_c0)), pl.BlockSpec(qk, bwd(nh))]
    args = [proj3, proj3, proj3, g3, proj3, proj3, proj3, g3]
    if rotary:
        cos, sin = _rope_tables(seq_len)
        tab = (GLA_BLOCK, GLA_DK)
        in_specs += [pl.BlockSpec(tab, lambda b, h, j: (j, 0)), pl.BlockSpec(tab, lambda b, h, j: (j, 0)),
                     pl.BlockSpec(tab, lambda b, h, j: (nblk - 1 - j, 0)), pl.BlockSpec(tab, lambda b, h, j: (nblk - 1 - j, 0))]
        args += [cos, sin, cos, sin]
    if has_init:
        st = (1, 1, GLA_DK, GLA_DV)
        in_specs += [pl.BlockSpec(st, lambda b, h, j: (b, h, 0, 0))] * 2
        args += [s_f0, s_b0]
    out_shape = [jax.ShapeDtypeStruct((n_seq, seq_len, D), F32)] * 2
    out_specs = [pl.BlockSpec(vv, lambda b, h, j: (b, j, h)),
                 pl.BlockSpec(vv, lambda b, h, j: (b, nblk - 1 - j, h))]
    if out_state:
        out_shape += [jax.ShapeDtypeStruct((n_seq, nh, GLA_DK, GLA_DV), F32)] * 2
        out_specs += [pl.BlockSpec((1, 1, GLA_DK, GLA_DV), lambda b, h, j: (b, h, 0, 0))] * 2
    res = pl.pallas_call(
        functools.partial(_gla_kernel, rotary=rotary, has_init=has_init, out_state=out_state, nblk=nblk),
        out_shape=out_shape,
        grid=(n_seq, nh, nblk),
        in_specs=in_specs,
        out_specs=out_specs,
        scratch_shapes=[pltpu.VMEM((GLA_DV, GLA_DK), F32)] * 2,
        compiler_params=_cparams(3),
        name="gla_scan_rope" if rotary else "gla_scan",
    )(*args)
    of, ob = res[0].reshape(n_seq * seq_len, D), res[1].reshape(n_seq * seq_len, D)
    if out_state:
        return of, ob, res[2], res[3]
    return of, ob


def _gla_out_kernel(ofc_ref, obc_ref, ofl_ref, obl_ref, r_ref, ng_ref, x_ref, m_ref, g_ref, w_ref, o_ref, wb_ref):
    i = pl.program_id(0)

    @pl.when(i == 0)
    def _():
        wb_ref[...] = w_ref[...].astype(BF16)

    o = jnp.where(i < CTX_TILES, ofc_ref[...] + obc_ref[...], ofl_ref[...] + obl_ref[...])
    r = r_ref[...]
    ng = ng_ref[...]
    parts = []
    for h in range(GLA_HEADS):
        oh = o[:, h * GLA_DV:(h + 1) * GLA_DV]
        parts.append(_rms(oh, ng))
    a = jnp.concatenate(parts, axis=1) * (r * jax.nn.sigmoid(r))
    y = jnp.dot(a.astype(BF16), wb_ref[...], preferred_element_type=F32)
    o_ref[...] = x_ref[...] + m_ref[0, 2:3, :] * _rms(y, g_ref[...])


def _gla_output(of_ctx, ob_ctx, of_lat, ob_lat, proj, norm_g, x, mod, g_post, w):
    return pl.pallas_call(
        _gla_out_kernel,
        out_shape=jax.ShapeDtypeStruct((T_ALL, D), F32),
        grid=(T_ALL // TM,),
        in_specs=[pl.BlockSpec((TM, D), _ctx_part),
                  pl.BlockSpec((TM, D), _ctx_part),
                  pl.BlockSpec((TM, D), _lat_part),
                  pl.BlockSpec((TM, D), _lat_part),
                  pl.BlockSpec((TM, D), lambda i: (i, 2)),
                  pl.BlockSpec((1, GLA_DV), lambda i: (0, 0)),
                  pl.BlockSpec((TM, D), lambda i: (i, 0)),
                  pl.BlockSpec((1, 6, D), lambda i: (_mod_idx(i, TM), 0, 0)),
                  pl.BlockSpec((1, D), lambda i: (0, 0)),
                  pl.BlockSpec((D, D), lambda i: (0, 0))],
        out_specs=pl.BlockSpec((TM, D), lambda i: (i, 0)),
        scratch_shapes=[pltpu.VMEM((D, D), BF16)],
        compiler_params=_cparams(1),
        name="gla_output",
    )(of_ctx, ob_ctx, of_lat, ob_lat, proj, norm_g.reshape(1, GLA_DV), x, mod, g_post.reshape(1, D), w)


def _router_kernel(x_ref, g_ref, m_ref, wr_ref, br_ref, h_ref, ri_ref, rw_ref, cnt_ref, cnt_scr):
    @pl.when(pl.program_id(0) == 0)
    def _():
        cnt_scr[...] = jnp.zeros_like(cnt_scr)

    h = _norm_mod(x_ref[...], g_ref[...], m_ref, 3)
    h_ref[...] = h
    logits = jnp.dot(h, wr_ref[...], precision=HIGHEST, preferred_element_type=F32) + br_ref[...]
    lane = lax.broadcasted_iota(I32, (TM, LANES), 1)
    lane_f = lane.astype(F32)
    cur = jnp.where(lane < N_EXPERTS, logits, -jnp.inf)
    vals, sels = [], []
    hot = jnp.zeros((TM, LANES), F32)
    for _ in range(TOP_K):
        m = jnp.max(cur, axis=-1, keepdims=True)
        idx = jnp.min(jnp.where(cur == m, lane_f, float(LANES)), axis=-1, keepdims=True)
        sel = lane_f == idx
        vals.append(m)
        sels.append((idx, sel))
        hot = hot + sel.astype(F32)
        cur = jnp.where(sel, -jnp.inf, cur)
    ex = [jnp.exp(v - vals[0]) for v in vals]
    den = ex[0] + ex[1] + ex[2] + ex[3]
    r_i = lax.broadcasted_iota(I32, (TM, TM), 0)
    c_i = lax.broadcasted_iota(I32, (TM, TM), 1)
    before = (c_i < r_i).astype(BF16)
    prefix = jnp.dot(before, hot.astype(BF16), preferred_element_type=F32) + cnt_scr[0:1, :]
    ri = jnp.zeros((TM, LANES), F32)
    rw = jnp.zeros((TM, LANES), F32)
    for k in range(TOP_K):
        idx, sel = sels[k]
        rank = jnp.sum(jnp.where(sel, prefix, 0.0), axis=-1, keepdims=True)
        ri = jnp.where(lane == k, idx, ri)
        ri = jnp.where(lane == TOP_K + k, rank, ri)
        rw = jnp.where(lane == k, ex[k] / den, rw)
    ri_ref[...] = ri.astype(I32)
    rw_ref[...] = rw
    cnt = cnt_scr[...] + jnp.sum(hot, axis=0, keepdims=True)
    cnt_scr[...] = cnt
    cnt_ref[...] = cnt


def _router(x, g, mod, w_router, b_router):
    wr = jnp.zeros((D, LANES), F32).at[:, :N_EXPERTS].set(w_router)
    br = jnp.zeros((1, LANES), F32).at[0, :N_EXPERTS].set(b_router)
    return pl.pallas_call(
        _router_kernel,
        out_shape=[jax.ShapeDtypeStruct((T_ALL, D), F32),
                   jax.ShapeDtypeStruct((T_ALL, LANES), I32),
                   jax.ShapeDtypeStruct((T_ALL, LANES), F32),
                   jax.ShapeDtypeStruct((8, LANES), F32)],
        grid=(T_ALL // TM,),
        in_specs=[pl.BlockSpec((TM, D), lambda i: (i, 0)),
                  pl.BlockSpec((1, D), lambda i: (0, 0)),
                  pl.BlockSpec((1, 6, D), lambda i: (_mod_idx(i, TM), 0, 0)),
                  pl.BlockSpec((D, LANES), lambda i: (0, 0)),
                  pl.BlockSpec((1, LANES), lambda i: (0, 0))],
        out_specs=[pl.BlockSpec((TM, D), lambda i: (i, 0)),
                   pl.BlockSpec((TM, LANES), lambda i: (i, 0)),
                   pl.BlockSpec((TM, LANES), lambda i: (i, 0)),
                   pl.BlockSpec((8, LANES), lambda i: (0, 0))],
        scratch_shapes=[pltpu.VMEM((8, LANES), F32)],
        compiler_params=_cparams(1),
        name="moe_router",
    )(x, g.reshape(1, D), mod, wr, br)


def _dispatch_kernel(dest_ref, h_ref, xs_ref, sem):
    base = pl.program_id(0) * (TD * TOP_K)

    def row_copy(i, k):
        d = dest_ref[base + i * TOP_K + k]
        return pltpu.make_async_copy(h_ref.at[pl.ds(i, 1)], xs_ref.at[pl.ds(d, 1)], sem)

    def start(i, c):
        for k in range(TOP_K):
            row_copy(i, k).start()
        return c

    def wait(i, c):
        for k in range(TOP_K):
            row_copy(i, k).wait()
        return c

    lax.fori_loop(0, TD, start, 0)
    lax.fori_loop(0, TD, wait, 0)


def _dispatch(dest, h):
    return pl.pallas_call(
        _dispatch_kernel,
        out_shape=jax.ShapeDtypeStruct((MOE_ROWS, D), F32),
        grid_spec=pltpu.PrefetchScalarGridSpec(
            num_scalar_prefetch=1,
            grid=(T_ALL // TD,),
            in_specs=[pl.BlockSpec((TD, D), lambda i, dest: (i, 0))],
            out_specs=pl.BlockSpec(memory_space=pl.ANY),
            scratch_shapes=[pltpu.SemaphoreType.DMA(())]),
        compiler_params=_cparams(1),
        name="moe_dispatch",
    )(dest, h)


def _gmm_kernel(vt_ref, vg_ref, lo_ref, hi_ref, first_ref, x_ref, wu_ref, bu_ref, wd_ref, bd_ref, y_ref, wub, wdb):
    v = pl.program_id(0)
    g = vg_ref[v]
    new_group = jnp.logical_or(v == 0, vg_ref[jnp.maximum(v - 1, 0)] != g)

    @pl.when(new_group)
    def _():
        wub[...] = wu_ref[0].astype(BF16)
        wdb[...] = wd_ref[0].astype(BF16)

    lo = lo_ref[v]
    hi = hi_ref[v]

    def expert_rows():
        u = jnp.dot(x_ref[...].astype(BF16), wub[...], preferred_element_type=F32) + bu_ref[0]
        gate = jnp.minimum(u[:, :D_FF], SWIGLU_LIMIT)
        lin = jnp.clip(u[:, D_FF:], -SWIGLU_LIMIT, SWIGLU_LIMIT)
        act = gate * jax.nn.sigmoid(SWIGLU_ALPHA * gate) * (lin + 1.0)
        y = jnp.dot(act.astype(BF16), wdb[...], preferred_element_type=F32) + bd_ref[0]
        row = lax.broadcasted_iota(I32, (MOE_TM, 1), 0)
        return y, jnp.logical_and(row >= lo, row < hi)

    @pl.when(jnp.logical_and(hi > lo, first_ref[v] == 1))
    def _():
        y, mine = expert_rows()
        y_ref[...] = jnp.where(mine, y, 0.0)

    @pl.when(jnp.logical_and(hi > lo, first_ref[v] == 0))
    def _():
        y, mine = expert_rows()
        y_ref[...] = jnp.where(mine, y, y_ref[...])


def _moe_visits(counts):
    ends = jnp.cumsum(counts)
    starts = ends - counts
    first_tile = starts // MOE_TM
    last_tile = (ends - 1) // MOE_TM
    ntl = jnp.where(counts > 0, last_tile - first_tile + 1, 0)
    vend = jnp.cumsum(ntl)
    vstart = vend - ntl
    total = vend[-1]
    v = jnp.arange(MOE_VISITS, dtype=I32)
    vc = jnp.minimum(v, total - 1)
    grp = jnp.minimum(jnp.searchsorted(vend, vc, side="right").astype(I32), N_EXPERTS - 1)
    tile = first_tile[grp] + (vc - vstart[grp])
    valid = v < total
    lo = jnp.where(valid, jnp.clip(starts[grp] - tile * MOE_TM, 0, MOE_TM), 0)
    hi = jnp.where(valid, jnp.clip(ends[grp] - tile * MOE_TM, 0, MOE_TM), 0)
    prev_tile = jnp.concatenate([jnp.full((1,), -1, I32), tile[:-1]])
    first = jnp.logical_and(valid, tile != prev_tile).astype(I32)
    return tile.astype(I32), grp, lo.astype(I32), hi.astype(I32), first, starts


def _grouped_mlp(visits, xs, w_up, b_up, w_down, b_down):
    tile, grp, lo, hi, first = visits
    return pl.pallas_call(
        _gmm_kernel,
        out_shape=jax.ShapeDtypeStruct((MOE_ROWS, D), F32),
        grid_spec=pltpu.PrefetchScalarGridSpec(
            num_scalar_prefetch=5,
            grid=(MOE_VISITS,),
            in_specs=[pl.BlockSpec((MOE_TM, D), lambda v, vt, vg, lo, hi, fi: (vt[v], 0)),
                      pl.BlockSpec((1, D, 2 * D_FF), lambda v, vt, vg, lo, hi, fi: (vg[v], 0, 0)),
                      pl.BlockSpec((1, 1, 2 * D_FF), lambda v, vt, vg, lo, hi, fi: (vg[v], 0, 0)),
                      pl.BlockSpec((1, D_FF, D), lambda v, vt, vg, lo, hi, fi: (vg[v], 0, 0)),
                      pl.BlockSpec((1, 1, D), lambda v, vt, vg, lo, hi, fi: (vg[v], 0, 0))],
            out_specs=pl.BlockSpec((MOE_TM, D), lambda v, vt, vg, lo, hi, fi: (vt[v], 0)),
            scratch_shapes=[pltpu.VMEM((D, 2 * D_FF), BF16), pltpu.VMEM((D_FF, D), BF16)]),
        compiler_params=_cparams(1),
        name="moe_grouped_mlp",
    )(tile, grp, lo, hi, first, xs, w_up, b_up.reshape(N_EXPERTS, 1, 2 * D_FF), w_down,
      b_down.reshape(N_EXPERTS, 1, D))


def _combine_kernel(dest_ref, ys_ref, rw_ref, x_ref, m_ref, g_ref, o_ref, buf, sem):
    base = pl.program_id(0) * (TC * TOP_K)

    def row_copy(i, k):
        d = dest_ref[base + i * TOP_K + k]
        return pltpu.make_async_copy(ys_ref.at[pl.ds(d, 1)], buf.at[k, pl.ds(i, 1)], sem)

    def start(i, c):
        for k in range(TOP_K):
            row_copy(i, k).start()
        return c

    def wait(i, c):
        for k in range(TOP_K):
            row_copy(i, k).wait()
        return c

    lax.fori_loop(0, TC, start, 0)
    lax.fori_loop(0, TC, wait, 0)
    w = rw_ref[...]
    f = w[:, 0:1] * buf[0]
    for k in range(1, TOP_K):
        f = f + w[:, k:k + 1] * buf[k]
    o_ref[...] = x_ref[...] + m_ref[0, 5:6, :] * _rms(f, g_ref[...])


def _combine(dest, ys, rw, x, mod, g_post):
    return pl.pallas_call(
        _combine_kernel,
        out_shape=jax.ShapeDtypeStruct((T_ALL, D), F32),
        grid_spec=pltpu.PrefetchScalarGridSpec(
            num_scalar_prefetch=1,
            grid=(T_ALL // TC,),
            in_specs=[pl.BlockSpec(memory_space=pl.ANY),
                      pl.BlockSpec((TC, LANES), lambda i, dest: (i, 0)),
                      pl.BlockSpec((TC, D), lambda i, dest: (i, 0)),
                      pl.BlockSpec((1, 6, D), lambda i, dest: (_mod_idx(i, TC), 0, 0)),
                      pl.BlockSpec((1, D), lambda i, dest: (0, 0))],
            out_specs=pl.BlockSpec((TC, D), lambda i, dest: (i, 0)),
            scratch_shapes=[pltpu.VMEM((TOP_K, TC, D), F32), pltpu.SemaphoreType.DMA(())]),
        compiler_params=_cparams(1),
        name="moe_combine",
    )(dest, ys, rw, x, mod, g_post.reshape(1, D))


def _moe(x, mod, g_pre, g_post, w_router, b_router, w_up, b_up, w_down, b_down):
    h, ri, rw, cnt = _router(x, g_pre, mod, w_router, b_router)
    counts = cnt[0, :N_EXPERTS].astype(I32)
    tile, grp, lo, hi, first, starts = _moe_visits(counts)
    dest = (starts[ri[:, :TOP_K]] + ri[:, TOP_K:2 * TOP_K]).reshape(-1).astype(I32)
    xs = _dispatch(dest, h)
    ys = _grouped_mlp((tile, grp, lo, hi, first), xs, w_up, b_up, w_down, b_down)
    return _combine(dest, ys, rw, x, mod, g_post)


def kernel(x_prompt, x_sample, cache_k, cache_v, state_fwd, state_bwd, c, c_ctx, w_ada, b_ada, g_pre_mix, g_post_mix, g_pre_ffn, g_post_ffn, na_w_qkv, na_rpb, na_w_out, gla_w_in, gla_w_g1_fwd, gla_w_g2_fwd, gla_b_g_fwd, gla_w_g1_bwd, gla_w_g2_bwd, gla_b_g_bwd, gla_norm_g, gla_w_out, moe_w_router, moe_b_router, moe_w_up, moe_b_up, moe_w_down, moe_b_down):
    x = jnp.concatenate([x_prompt.reshape(T_CTX, D), x_sample.reshape(T_LAT, D)], axis=0)
    cond = jnp.concatenate([c_ctx[None, :], c, jnp.zeros((8 - 1 - N_LAT_SEQ, D), F32)], axis=0)
    mods = _modulation(cond, w_ada, b_ada)

    qkv = _norm_mod_matmul(x, g_pre_mix[0], mods[0], na_w_qkv[0], 0, D)
    a_ctx = _ctx_attention(qkv)
    a_lat = _na_attention(qkv, cache_k[:, 0], cache_v[:, 0], na_rpb[0])
    x = _proj_residual(a_ctx, a_lat, x, mods[0], g_post_mix[0], na_w_out[0], 2)
    new_k = qkv[:T_CTX, D:2 * D].reshape(N_CTX_SEQ, 1, CTX_LEN, NA_HEADS, NA_HD)
    new_v = qkv[:T_CTX, 2 * D:].reshape(N_CTX_SEQ, 1, CTX_LEN, NA_HEADS, NA_HD)
    x = _moe(x, mods[0], g_pre_ffn[0], g_post_ffn[0], moe_w_router[0], moe_b_router[0],
             moe_w_up[0], moe_b_up[0], moe_w_down[0], moe_b_down[0])

    proj = _norm_mod_matmul(x, g_pre_mix[1], mods[1], gla_w_in[0], 0, D)
    gates = _gla_gates(x, g_pre_mix[1], mods[1], gla_w_g1_fwd[0], gla_w_g2_fwd[0], gla_b_g_fwd[0],
                       gla_w_g1_bwd[0], gla_w_g2_bwd[0], gla_b_g_bwd[0])
    of_c, ob_c, s_f, s_b = _gla_scan(proj, gates, seq_len=CTX_LEN, n_seq=N_CTX_SEQ, seq_off=0, out_state=True)
    of_l, ob_l = _gla_scan(proj, gates, seq_len=LAT_LEN, n_seq=N_LAT_SEQ, seq_off=T_CTX // LAT_LEN,
                           s_f0=state_fwd[:, 0], s_b0=state_bwd[:, 0], rotary=True)
    x = _gla_output(of_c, ob_c, of_l, ob_l, proj, gla_norm_g[0], x, mods[1], g_post_mix[1], gla_w_out[0])
    x = _moe(x, mods[1], g_pre_ffn[1], g_post_ffn[1], moe_w_router[1], moe_b_router[1],
             moe_w_up[1], moe_b_up[1], moe_w_down[1], moe_b_down[1])

    return (x[:T_CTX].reshape(N_CTX_SEQ, CTX_LEN, D), x[T_CTX:].reshape(N_LAT_SEQ, LAT_LEN, D),
            new_k, new_v, s_f[:, None], s_b[:, None])
```

```python
import functools

import numpy as np
import jax
import jax.numpy as jnp
from jax import lax
from jax.experimental import pallas as pl
from jax.experimental.pallas import tpu as pltpu

F32 = jnp.float32
BF16 = jnp.bfloat16
I32 = jnp.int32
HIGHEST = lax.Precision.HIGHEST

D = 1024
N_CTX_SEQ = 32
CTX_LEN = 256
N_LAT_SEQ = 2
LAT_LEN = 4096
T_CTX = N_CTX_SEQ * CTX_LEN
T_LAT = N_LAT_SEQ * LAT_LEN
T_ALL = T_CTX + T_LAT
GRID_W = 64
GRID_ROWS = LAT_LEN // GRID_W
NA_HEADS = 16
NA_HD = 64
NA_WIN_ROWS = 8
NA_WIN_COLS = 16
GLA_HEADS = 4
GLA_DK = 128
GLA_DV = 256
GLA_RANK = 16
GLA_GATE_NORM = 16.0
GLA_CHUNK = 64
ROPE_THETA = 10000.0
N_EXPERTS = 32
TOP_K = 4
D_FF = 1024
SWIGLU_LIMIT = 7.0
SWIGLU_ALPHA = 1.702
EPS = 1e-6
NEG_INF = -1e30

LANES = 128
TM = 512
GLA_BLOCK = 256
MOE_TM = 512
MOE_ROWS = T_ALL * TOP_K
MOE_TILES = MOE_ROWS // MOE_TM
MOE_VISITS = MOE_TILES + N_EXPERTS - 1
TD = 512
TC = 256
DMA_UNROLL = 8
VMEM_LIMIT = 60 * 1024 * 1024

_NT = (((1,), (1,)), ((), ()))
_TN = (((0,), (0,)), ((), ()))


def _cparams(n_axes, vmem=None):
    return pltpu.CompilerParams(
        dimension_semantics=("arbitrary",) * n_axes,
        vmem_limit_bytes=VMEM_LIMIT if vmem is None else vmem)


def _mod_idx(i, tm):
    return jnp.maximum((i * tm) // LAT_LEN - 1, 0)


def _rms(x, g):
    return x * lax.rsqrt(jnp.mean(x * x, axis=-1, keepdims=True) + EPS) * g


def _norm_mod(x, g, m_ref, shift_row):
    sh = m_ref[0, shift_row:shift_row + 1, :]
    sc = m_ref[0, shift_row + 1:shift_row + 2, :]
    return _rms(x, g) * (1.0 + sc) + sh


def _mod_kernel(c_ref, w_ref, b_ref, o_ref):
    c = c_ref[...]
    s = c * jax.nn.sigmoid(c)
    o_ref[0] = jnp.dot(s, w_ref[0], precision=HIGHEST, preferred_element_type=F32) + b_ref[0]


def _modulation(cond, w_ada, b_ada):
    depth = w_ada.shape[0]
    out = pl.pallas_call(
        _mod_kernel,
        out_shape=jax.ShapeDtypeStruct((depth, 8, 6 * D), F32),
        grid=(depth, 6),
        in_specs=[pl.BlockSpec((8, D), lambda l, j: (0, 0)),
                  pl.BlockSpec((1, D, D), lambda l, j: (l, 0, j)),
                  pl.BlockSpec((1, 1, D), lambda l, j: (l, 0, j))],
        out_specs=pl.BlockSpec((1, 8, D), lambda l, j: (l, 0, j)),
        compiler_params=_cparams(2),
        name="adaln_modulation",
    )(cond, w_ada, b_ada.reshape(depth, 1, 6 * D))
    return out.reshape(depth, 8, 6, D)


def _nmm_kernel(x_ref, g_ref, m_ref, w_ref, o_ref, wb_ref, *, shift_row):
    @pl.when(pl.program_id(1) == 0)
    def _():
        wb_ref[...] = w_ref[...].astype(BF16)

    h = _norm_mod(x_ref[...], g_ref[...], m_ref, shift_row)
    o_ref[...] = jnp.dot(h.astype(BF16), wb_ref[...], preferred_element_type=F32)


def _norm_mod_matmul(x, g, mod, w, shift_row, tn):
    t, n = x.shape[0], w.shape[1]
    return pl.pallas_call(
        functools.partial(_nmm_kernel, shift_row=shift_row),
        out_shape=jax.ShapeDtypeStruct((t, n), F32),
        grid=(n // tn, t // TM),
        in_specs=[pl.BlockSpec((TM, D), lambda j, i: (i, 0)),
                  pl.BlockSpec((1, D), lambda j, i: (0, 0)),
                  pl.BlockSpec((1, 6, D), lambda j, i: (_mod_idx(i, TM), 0, 0)),
                  pl.BlockSpec((D, tn), lambda j, i: (0, j))],
        out_specs=pl.BlockSpec((TM, tn), lambda j, i: (i, j)),
        scratch_shapes=[pltpu.VMEM((D, tn), BF16)],
        compiler_params=_cparams(2),
        name="norm_mod_matmul",
    )(x, g.reshape(1, D), mod, w)


def _ctx_attn_kernel(qkv_ref, o_ref):
    lane = lax.broadcasted_iota(I32, (1, LANES), 1)
    scale = NA_HD ** -0.5
    for hp in range(NA_HEADS // 2):
        q = qkv_ref[0, :, hp * LANES:(hp + 1) * LANES] * scale
        k = qkv_ref[0, :, D + hp * LANES:D + (hp + 1) * LANES].astype(BF16)
        v = qkv_ref[0, :, 2 * D + hp * LANES:2 * D + (hp + 1) * LANES]
        acc = jnp.zeros((CTX_LEN, LANES), F32)
        for half in range(2):
            msk = (lane < NA_HD) if half == 0 else (lane >= NA_HD)
            qm = jnp.where(msk, q, 0.0).astype(BF16)
            s = lax.dot_general(qm, k, _NT, preferred_element_type=F32)
            p = jnp.exp(s - jnp.max(s, axis=-1, keepdims=True))
            l = jnp.sum(p, axis=-1, keepdims=True)
            vm = jnp.where(msk, v, 0.0).astype(BF16)
            acc = acc + jnp.dot(p.astype(BF16), vm, preferred_element_type=F32) / l
        o_ref[0, :, hp * LANES:(hp + 1) * LANES] = acc


def _ctx_attention(qkv):
    out = pl.pallas_call(
        _ctx_attn_kernel,
        out_shape=jax.ShapeDtypeStruct((N_CTX_SEQ, CTX_LEN, D), F32),
        grid=(N_CTX_SEQ,),
        in_specs=[pl.BlockSpec((1, CTX_LEN, 3 * D), lambda b: (b, 0, 0))],
        out_specs=pl.BlockSpec((1, CTX_LEN, D), lambda b: (b, 0, 0)),
        compiler_params=_cparams(1),
        name="context_attention",
    )(qkv.reshape(T_ALL // CTX_LEN, CTX_LEN, 3 * D))
    return out.reshape(T_CTX, D)


def _na_kernel(q_ref, k_ref, v_ref, ck_ref, cv_ref, bias_ref, o_ref):
    lane = lax.broadcasted_iota(I32, (1, LANES), 1)
    masks = ((lane < NA_HD), (lane >= NA_HD))
    scale = NA_HD ** -0.5
    ck = ck_ref[0].astype(BF16)
    cv = cv_ref[0]
    cvm = [jnp.where(m, cv, 0.0).astype(BF16) for m in masks]
    nwin = NA_WIN_ROWS * GRID_W

    def body(r, carry):
        r0 = jnp.clip(r - NA_WIN_ROWS // 2, 0, GRID_ROWS - NA_WIN_ROWS)
        base = r0 - r + NA_WIN_ROWS - 1
        qs = pl.ds(pl.multiple_of(r * GRID_W, GRID_W), GRID_W)
        ws = pl.ds(pl.multiple_of(r0 * GRID_W, GRID_W), nwin)
        q = q_ref[0, qs, :] * scale
        kw = k_ref[0, ws, :].astype(BF16)
        vw = v_ref[0, ws, :]
        acc = jnp.zeros((GRID_W, LANES), F32)
        for half in range(2):
            qm = jnp.where(masks[half], q, 0.0).astype(BF16)
            sw = lax.dot_general(qm, kw, _NT, preferred_element_type=F32) + bias_ref[half, pl.ds(base, 1)][0]
            sc = lax.dot_general(qm, ck, _NT, preferred_element_type=F32)
            m = jnp.maximum(jnp.max(sw, axis=-1, keepdims=True), jnp.max(sc, axis=-1, keepdims=True))
            pw = jnp.exp(sw - m)
            pc = jnp.exp(sc - m)
            l = jnp.sum(pw, axis=-1, keepdims=True) + jnp.sum(pc, axis=-1, keepdims=True)
            vm = jnp.where(masks[half], vw, 0.0).astype(BF16)
            o = (jnp.dot(pw.astype(BF16), vm, preferred_element_type=F32)
                 + jnp.dot(pc.astype(BF16), cvm[half], preferred_element_type=F32))
            acc = acc + o / l
        o_ref[0, qs, :] = acc
        return carry

    lax.fori_loop(0, GRID_ROWS, body, 0)


def _na_bias_table(rpb):
    cidx = np.arange(GRID_W)
    col_start = np.clip(cidx - NA_WIN_COLS // 2, 0, GRID_W - NA_WIN_COLS)
    col_ok = (cidx[None, :] >= col_start[:, None]) & (cidx[None, :] < col_start[:, None] + NA_WIN_COLS)
    coff = np.clip(cidx[None, :] - cidx[:, None], -(NA_WIN_COLS - 1), NA_WIN_COLS - 1) + NA_WIN_COLS - 1
    n_coff = 2 * NA_WIN_COLS - 1
    onehot = (coff[None, :, :] == np.arange(n_coff)[:, None, None]).astype(np.float32)
    tab = jnp.einsum("hrj,jcw->hrcw", rpb.astype(F32), onehot, precision=HIGHEST)
    tab = jnp.where(col_ok[None, None], tab, NEG_INF)
    return jnp.stack([jnp.concatenate([tab[:, b + j] for j in range(NA_WIN_ROWS)], axis=-1)
                      for b in range(NA_WIN_ROWS)], axis=1)


def _na_attention(qkv, cache_k, cache_v, rpb):
    off = T_CTX // LAT_LEN
    qkv3 = qkv.reshape(T_ALL // LAT_LEN, LAT_LEN, 3 * D)
    ck = cache_k.reshape(N_LAT_SEQ, CTX_LEN, D)
    cv = cache_v.reshape(N_LAT_SEQ, CTX_LEN, D)
    bias = _na_bias_table(rpb)
    nh = D // LANES
    out = pl.pallas_call(
        _na_kernel,
        out_shape=jax.ShapeDtypeStruct((N_LAT_SEQ, LAT_LEN, D), F32),
        grid=(N_LAT_SEQ, nh),
        in_specs=[pl.BlockSpec((1, LAT_LEN, LANES), lambda b, h: (b + off, 0, h)),
                  pl.BlockSpec((1, LAT_LEN, LANES), lambda b, h: (b + off, 0, nh + h)),
                  pl.BlockSpec((1, LAT_LEN, LANES), lambda b, h: (b + off, 0, 2 * nh + h)),
                  pl.BlockSpec((1, CTX_LEN, LANES), lambda b, h: (b, 0, h)),
                  pl.BlockSpec((1, CTX_LEN, LANES), lambda b, h: (b, 0, h)),
                  pl.BlockSpec((2, NA_WIN_ROWS, GRID_W, NA_WIN_ROWS * GRID_W), lambda b, h: (h, 0, 0, 0))],
        out_specs=pl.BlockSpec((1, LAT_LEN, LANES), lambda b, h: (b, 0, h)),
        compiler_params=_cparams(2),
        name="neighbourhood_attention",
    )(qkv3, qkv3, qkv3, ck, cv, bias)
    return out.reshape(T_LAT, D)


CTX_TILES = T_CTX // TM


def _ctx_part(i):
    return (jnp.minimum(i, CTX_TILES - 1), 0)


def _lat_part(i):
    return (jnp.maximum(i - CTX_TILES, 0), 0)


def _proj_res_kernel(ac_ref, al_ref, x_ref, m_ref, g_ref, w_ref, o_ref, wb_ref, *, gate_row):
    i = pl.program_id(0)

    @pl.when(i == 0)
    def _():
        wb_ref[...] = w_ref[...].astype(BF16)

    a = jnp.where(i < CTX_TILES, ac_ref[...], al_ref[...])
    y = jnp.dot(a.astype(BF16), wb_ref[...], preferred_element_type=F32)
    o_ref[...] = x_ref[...] + m_ref[0, gate_row:gate_row + 1, :] * _rms(y, g_ref[...])


def _proj_residual(a_ctx, a_lat, x, mod, g_post, w, gate_row):
    return pl.pallas_call(
        functools.partial(_proj_res_kernel, gate_row=gate_row),
        out_shape=jax.ShapeDtypeStruct((T_ALL, D), F32),
        grid=(T_ALL // TM,),
        in_specs=[pl.BlockSpec((TM, D), _ctx_part),
                  pl.BlockSpec((TM, D), _lat_part),
                  pl.BlockSpec((TM, D), lambda i: (i, 0)),
                  pl.BlockSpec((1, 6, D), lambda i: (_mod_idx(i, TM), 0, 0)),
                  pl.BlockSpec((1, D), lambda i: (0, 0)),
                  pl.BlockSpec((D, D), lambda i: (0, 0))],
        out_specs=pl.BlockSpec((TM, D), lambda i: (i, 0)),
        scratch_shapes=[pltpu.VMEM((D, D), BF16)],
        compiler_params=_cparams(1),
        name="proj_residual",
    )(a_ctx, a_lat, x, mod, g_post.reshape(1, D), w)


def _gla_gate_kernel(x_ref, g_ref, m_ref, w1_ref, w2_ref, b_ref, o_ref):
    h = _norm_mod(x_ref[...], g_ref[...], m_ref, 0)
    z = jnp.dot(h.astype(BF16), w1_ref[...].astype(BF16), preferred_element_type=F32)
    y = jnp.dot(z, w2_ref[...], precision=HIGHEST, preferred_element_type=F32) + b_ref[...]
    o_ref[...] = (jnp.minimum(y, 0.0) - jnp.log(1.0 + jnp.exp(-jnp.abs(y)))) * (1.0 / GLA_GATE_NORM)


def _gla_gates(x, g, mod, w1f, w2f, bf, w1b, w2b, bb):
    hk = GLA_HEADS * GLA_DK
    w1 = jnp.zeros((D, LANES), F32).at[:, :GLA_RANK].set(w1f).at[:, GLA_RANK:2 * GLA_RANK].set(w1b)
    w2 = jnp.zeros((LANES, 2 * hk), F32).at[:GLA_RANK, :hk].set(w2f).at[GLA_RANK:2 * GLA_RANK, hk:].set(w2b)
    b = jnp.concatenate([bf, bb]).reshape(1, 2 * hk)
    return pl.pallas_call(
        _gla_gate_kernel,
        out_shape=jax.ShapeDtypeStruct((T_ALL, 2 * hk), F32),
        grid=(T_ALL // TM,),
        in_specs=[pl.BlockSpec((TM, D), lambda i: (i, 0)),
                  pl.BlockSpec((1, D), lambda i: (0, 0)),
                  pl.BlockSpec((1, 6, D), lambda i: (_mod_idx(i, TM), 0, 0)),
                  pl.BlockSpec((D, LANES), lambda i: (0, 0)),
                  pl.BlockSpec((LANES, 2 * hk), lambda i: (0, 0)),
                  pl.BlockSpec((1, 2 * hk), lambda i: (0, 0))],
        out_specs=pl.BlockSpec((TM, 2 * hk), lambda i: (i, 0)),
        compiler_params=_cparams(1),
        name="gla_gates",
    )(x, g.reshape(1, D), mod, w1, w2, b)


def _rope(x, cos, sin_signed):
    lane = lax.broadcasted_iota(I32, (1, LANES), 1)
    partner = jnp.where((lane % 64) < 32, pltpu.roll(x, LANES - 32, 1), pltpu.roll(x, 32, 1))
    return x * cos + partner * sin_signed


def _gla_chunk(q, k, v, g, st, tri, forward):
    L = GLA_CHUNK
    cum = jnp.dot(tri.astype(F32), g, precision=HIGHEST, preferred_element_type=F32)
    cl = cum[L - 1:L, :] if forward else cum[0:1, :]
    q_dec = (q * (GLA_DK ** -0.5) * jnp.exp(cum)).astype(BF16)
    k_dec = (k * jnp.exp(-cum)).astype(BF16)
    k_rem = (k * jnp.exp(cl - cum)).astype(BF16)
    vb = v.astype(BF16)
    a = lax.dot_general(q_dec, k_dec, _NT, preferred_element_type=F32)
    a = jnp.where(tri, a, 0.0).astype(BF16)
    o = (jnp.dot(a, vb, preferred_element_type=F32)
         + lax.dot_general(q_dec, st.astype(BF16), _NT, preferred_element_type=F32))
    kv_t = lax.dot_general(vb, k_rem, _TN, preferred_element_type=F32)
    return o, jnp.exp(cl) * st + kv_t


def _gla_kernel(*refs, rotary, has_init, out_state, nblk):
    refs = list(refs)
    qf, kf, vf, gf, qb, kb, vb, gb = refs[:8]
    del refs[:8]
    if rotary:
        cosf, sinf, cosb, sinb = refs[:4]
        del refs[:4]
    if has_init:
        sf0, sb0 = refs[:2]
        del refs[:2]
    of, ob = refs[:2]
    del refs[:2]
    if out_state:
        sfo, sbo = refs[:2]
        del refs[:2]
    st_f, st_b = refs

    j = pl.program_id(2)
    L = GLA_CHUNK

    @pl.when(j == 0)
    def _():
        if has_init:
            st_f[...] = sf0[0, 0].T
            st_b[...] = sb0[0, 0].T
        else:
            st_f[...] = jnp.zeros_like(st_f)
            st_b[...] = jnp.zeros_like(st_b)

    row = lax.broadcasted_iota(I32, (L, L), 0)
    col = lax.broadcasted_iota(I32, (L, L), 1)
    tri_f = col <= row
    tri_b = col >= row

    nchunk = GLA_BLOCK // L
    s = st_f[...]
    for c in range(nchunk):
        sl = slice(c * L, (c + 1) * L)
        q, k = qf[0, sl, :], kf[0, sl, :]
        if rotary:
            q = _rope(q, cosf[sl, :], sinf[sl, :])
            k = _rope(k, cosf[sl, :], sinf[sl, :])
        o, s = _gla_chunk(q, k, vf[0, sl, :], gf[0, sl, :], s, tri_f, True)
        of[0, sl, :] = o
    st_f[...] = s

    s = st_b[...]
    for c in reversed(range(nchunk)):
        sl = slice(c * L, (c + 1) * L)
        q, k = qb[0, sl, :], kb[0, sl, :]
        if rotary:
            q = _rope(q, cosb[sl, :], sinb[sl, :])
            k = _rope(k, cosb[sl, :], sinb[sl, :])
        o, s = _gla_chunk(q, k, vb[0, sl, :], gb[0, sl, :], s, tri_b, False)
        ob[0, sl, :] = o
    st_b[...] = s

    if out_state:
        @pl.when(j == nblk - 1)
        def _():
            sfo[0, 0] = st_f[...].T
            sbo[0, 0] = st_b[...].T


def _rope_tables(n):
    t = np.arange(n)
    n_freq = GLA_DK // 4
    inv = ROPE_THETA ** (-np.arange(n_freq, dtype=np.float64) / n_freq)
    ang_r = (t // GRID_W).astype(np.float64)[:, None] * inv[None, :]
    ang_c = (t % GRID_W).astype(np.float64)[:, None] * inv[None, :]
    cos = np.concatenate([np.cos(ang_r), np.cos(ang_r), np.cos(ang_c), np.cos(ang_c)], axis=1)
    sin = np.concatenate([-np.sin(ang_r), np.sin(ang_r), -np.sin(ang_c), np.sin(ang_c)], axis=1)
    return jnp.asarray(cos, F32), jnp.asarray(sin, F32)


def _gla_scan(proj, gates, *, seq_len, n_seq, seq_off, s_f0=None, s_b0=None, rotary=False, out_state=False):
    nseq_all = T_ALL // seq_len
    nblk = seq_len // GLA_BLOCK
    nh = GLA_HEADS
    proj3 = proj.reshape(nseq_all, seq_len, 3 * D)
    g3 = gates.reshape(nseq_all, seq_len, 2 * nh * GLA_DK)
    has_init = s_f0 is not None

    def fwd(c0):
        return lambda b, h, j: (b + seq_off, j, c0 + h)

    def bwd(c0):
        return lambda b, h, j: (b + seq_off, nblk - 1 - j, c0 + h)

    qk = (1, GLA_BLOCK, GLA_DK)
    vv = (1, GLA_BLOCK, GLA_DV)
    v_c0 = 2 * nh * GLA_DK // GLA_DV
    in_specs = [pl.BlockSpec(qk, fwd(0)), pl.BlockSpec(qk, fwd(nh)), pl.BlockSpec(vv, fwd(v_c0)), pl.BlockSpec(qk, fwd(0)),
                pl.BlockSpec(qk, bwd(0)), pl.BlockSpec(qk, bwd(nh)), pl.BlockSpec(vv, bwd(v_c0)), pl.BlockSpec(qk, bwd(nh))]
    args = [proj3, proj3, proj3, g3, proj3, proj3, proj3, g3]
    if rotary:
        cos, sin = _rope_tables(seq_len)
        tab = (GLA_BLOCK, GLA_DK)
        in_specs += [pl.BlockSpec(tab, lambda b, h, j: (j, 0)), pl.BlockSpec(tab, lambda b, h, j: (j, 0)),
                     pl.BlockSpec(tab, lambda b, h, j: (nblk - 1 - j, 0)), pl.BlockSpec(tab, lambda b, h, j: (nblk - 1 - j, 0))]
        args += [cos, sin, cos, sin]
    if has_init:
        st = (1, 1, GLA_DK, GLA_DV)
        in_specs += [pl.BlockSpec(st, lambda b, h, j: (b, h, 0, 0))] * 2
        args += [s_f0, s_b0]
    out_shape = [jax.ShapeDtypeStruct((n_seq, seq_len, D), F32)] * 2
    out_specs = [pl.BlockSpec(vv, lambda b, h, j: (b, j, h)),
                 pl.BlockSpec(vv, lambda b, h, j: (b, nblk - 1 - j, h))]
    if out_state:
        out_shape += [jax.ShapeDtypeStruct((n_seq, nh, GLA_DK, GLA_DV), F32)] * 2
        out_specs += [pl.BlockSpec((1, 1, GLA_DK, GLA_DV), lambda b, h, j: (b, h, 0, 0))] * 2
    res = pl.pallas_call(
        functools.partial(_gla_kernel, rotary=rotary, has_init=has_init, out_state=out_state, nblk=nblk),
        out_shape=out_shape,
        grid=(n_seq, nh, nblk),
        in_specs=in_specs,
        out_specs=out_specs,
        scratch_shapes=[pltpu.VMEM((GLA_DV, GLA_DK), F32)] * 2,
        compiler_params=_cparams(3),
        name="gla_scan_rope" if rotary else "gla_scan",
    )(*args)
    of, ob = res[0].reshape(n_seq * seq_len, D), res[1].reshape(n_seq * seq_len, D)
    if out_state:
        return of, ob, res[2], res[3]
    return of, ob


def _gla_out_kernel(ofc_ref, obc_ref, ofl_ref, obl_ref, r_ref, ng_ref, x_ref, m_ref, g_ref, w_ref, o_ref, wb_ref):
    i = pl.program_id(0)

    @pl.when(i == 0)
    def _():
        wb_ref[...] = w_ref[...].astype(BF16)

    o = jnp.where(i < CTX_TILES, ofc_ref[...] + obc_ref[...], ofl_ref[...] + obl_ref[...])
    r = r_ref[...]
    ng = ng_ref[...]
    parts = []
    for h in range(GLA_HEADS):
        oh = o[:, h * GLA_DV:(h + 1) * GLA_DV]
        parts.append(_rms(oh, ng))
    a = jnp.concatenate(parts, axis=1) * (r * jax.nn.sigmoid(r))
    y = jnp.dot(a.astype(BF16), wb_ref[...], preferred_element_type=F32)
    o_ref[...] = x_ref[...] + m_ref[0, 2:3, :] * _rms(y, g_ref[...])


def _gla_output(of_ctx, ob_ctx, of_lat, ob_lat, proj, norm_g, x, mod, g_post, w):
    return pl.pallas_call(
        _gla_out_kernel,
        out_shape=jax.ShapeDtypeStruct((T_ALL, D), F32),
        grid=(T_ALL // TM,),
        in_specs=[pl.BlockSpec((TM, D), _ctx_part),
                  pl.BlockSpec((TM, D), _ctx_part),
                  pl.BlockSpec((TM, D), _lat_part),
                  pl.BlockSpec((TM, D), _lat_part),
                  pl.BlockSpec((TM, D), lambda i: (i, 2)),
                  pl.BlockSpec((1, GLA_DV), lambda i: (0, 0)),
                  pl.BlockSpec((TM, D), lambda i: (i, 0)),
                  pl.BlockSpec((1, 6, D), lambda i: (_mod_idx(i, TM), 0, 0)),
                  pl.BlockSpec((1, D), lambda i: (0, 0)),
                  pl.BlockSpec((D, D), lambda i: (0, 0))],
        out_specs=pl.BlockSpec((TM, D), lambda i: (i, 0)),
        scratch_shapes=[pltpu.VMEM((D, D), BF16)],
        compiler_params=_cparams(1),
        name="gla_output",
    )(of_ctx, ob_ctx, of_lat, ob_lat, proj, norm_g.reshape(1, GLA_DV), x, mod, g_post.reshape(1, D), w)


def _router_kernel(x_ref, g_ref, m_ref, wr_ref, br_ref, h_ref, ri_ref, rw_ref, cnt_ref, cnt_scr):
    @pl.when(pl.program_id(0) == 0)
    def _():
        cnt_scr[...] = jnp.zeros_like(cnt_scr)

    h = _norm_mod(x_ref[...], g_ref[...], m_ref, 3)
    h_ref[...] = h
    logits = jnp.dot(h, wr_ref[...], precision=HIGHEST, preferred_element_type=F32) + br_ref[...]
    lane = lax.broadcasted_iota(I32, (TM, LANES), 1)
    lane_f = lane.astype(F32)
    cur = jnp.where(lane < N_EXPERTS, logits, -jnp.inf)
    vals, sels = [], []
    hot = jnp.zeros((TM, LANES), F32)
    for _ in range(TOP_K):
        m = jnp.max(cur, axis=-1, keepdims=True)
        idx = jnp.min(jnp.where(cur == m, lane_f, float(LANES)), axis=-1, keepdims=True)
        sel = lane_f == idx
        vals.append(m)
        sels.append((idx, sel))
        hot = hot + sel.astype(F32)
        cur = jnp.where(sel, -jnp.inf, cur)
    ex = [jnp.exp(v - vals[0]) for v in vals]
    den = ex[0] + ex[1] + ex[2] + ex[3]
    r_i = lax.broadcasted_iota(I32, (TM, TM), 0)
    c_i = lax.broadcasted_iota(I32, (TM, TM), 1)
    before = (c_i < r_i).astype(BF16)
    prefix = jnp.dot(before, hot.astype(BF16), preferred_element_type=F32) + cnt_scr[0:1, :]
    ri = jnp.zeros((TM, LANES), F32)
    rw = jnp.zeros((TM, LANES), F32)
    for k in range(TOP_K):
        idx, sel = sels[k]
        rank = jnp.sum(jnp.where(sel, prefix, 0.0), axis=-1, keepdims=True)
        ri = jnp.where(lane == k, idx, ri)
        ri = jnp.where(lane == TOP_K + k, rank, ri)
        rw = jnp.where(lane == k, ex[k] / den, rw)
    ri_ref[...] = ri.astype(I32)
    rw_ref[...] = rw
    cnt = cnt_scr[...] + jnp.sum(hot, axis=0, keepdims=True)
    cnt_scr[...] = cnt
    cnt_ref[...] = cnt


def _router(x, g, mod, w_router, b_router):
    wr = jnp.zeros((D, LANES), F32).at[:, :N_EXPERTS].set(w_router)
    br = jnp.zeros((1, LANES), F32).at[0, :N_EXPERTS].set(b_router)
    return pl.pallas_call(
        _router_kernel,
        out_shape=[jax.ShapeDtypeStruct((T_ALL, D), F32),
                   jax.ShapeDtypeStruct((T_ALL, LANES), I32),
                   jax.ShapeDtypeStruct((T_ALL, LANES), F32),
                   jax.ShapeDtypeStruct((8, LANES), F32)],
        grid=(T_ALL // TM,),
        in_specs=[pl.BlockSpec((TM, D), lambda i: (i, 0)),
                  pl.BlockSpec((1, D), lambda i: (0, 0)),
                  pl.BlockSpec((1, 6, D), lambda i: (_mod_idx(i, TM), 0, 0)),
                  pl.BlockSpec((D, LANES), lambda i: (0, 0)),
                  pl.BlockSpec((1, LANES), lambda i: (0, 0))],
        out_specs=[pl.BlockSpec((TM, D), lambda i: (i, 0)),
                   pl.BlockSpec((TM, LANES), lambda i: (i, 0)),
                   pl.BlockSpec((TM, LANES), lambda i: (i, 0)),
                   pl.BlockSpec((8, LANES), lambda i: (0, 0))],
        scratch_shapes=[pltpu.VMEM((8, LANES), F32)],
        compiler_params=_cparams(1),
        name="moe_router",
    )(x, g.reshape(1, D), mod, wr, br)


def _dispatch_kernel(dest_ref, h_ref, xs_ref, sem):
    base = pl.program_id(0) * (TD * TOP_K)

    def row_copy(i, k):
        d = dest_ref[base + i * TOP_K + k]
        return pltpu.make_async_copy(h_ref.at[pl.ds(i, 1)], xs_ref.at[pl.ds(d, 1)], sem)

    def start(i, c):
        for k in range(TOP_K):
            row_copy(i, k).start(priority=k % 2)
        return c

    def wait(i, c):
        for k in range(TOP_K):
            row_copy(i, k).wait()
        return c

    lax.fori_loop(0, TD, start, 0, unroll=DMA_UNROLL)
    lax.fori_loop(0, TD, wait, 0, unroll=DMA_UNROLL)


def _dispatch(dest, h):
    return pl.pallas_call(
        _dispatch_kernel,
        out_shape=jax.ShapeDtypeStruct((MOE_ROWS, D), F32),
        grid_spec=pltpu.PrefetchScalarGridSpec(
            num_scalar_prefetch=1,
            grid=(T_ALL // TD,),
            in_specs=[pl.BlockSpec((TD, D), lambda i, dest: (i, 0))],
            out_specs=pl.BlockSpec(memory_space=pl.ANY),
            scratch_shapes=[pltpu.SemaphoreType.DMA(())]),
        compiler_params=_cparams(1),
        name="moe_dispatch",
    )(dest, h)


def _gmm_kernel(vt_ref, vg_ref, lo_ref, hi_ref, first_ref, x_ref, wu_ref, bu_ref, wd_ref, bd_ref, y_ref, wub, wdb):
    v = pl.program_id(0)
    g = vg_ref[v]
    new_group = jnp.logical_or(v == 0, vg_ref[jnp.maximum(v - 1, 0)] != g)

    @pl.when(new_group)
    def _():
        wub[...] = wu_ref[0, 0].astype(BF16)
        wdb[...] = wd_ref[0, 0].astype(BF16)

    lo = lo_ref[v]
    hi = hi_ref[v]

    def expert_rows():
        u = jnp.dot(x_ref[...].astype(BF16), wub[...], preferred_element_type=F32) + bu_ref[0, 0]
        gate = jnp.minimum(u[:, :D_FF], SWIGLU_LIMIT)
        lin = jnp.clip(u[:, D_FF:], -SWIGLU_LIMIT, SWIGLU_LIMIT)
        act = gate * jax.nn.sigmoid(SWIGLU_ALPHA * gate) * (lin + 1.0)
        y = jnp.dot(act.astype(BF16), wdb[...], preferred_element_type=F32) + bd_ref[0, 0]
        row = lax.broadcasted_iota(I32, (MOE_TM, 1), 0)
        return y, jnp.logical_and(row >= lo, row < hi)

    @pl.when(jnp.logical_and(hi > lo, first_ref[v] == 1))
    def _():
        y, mine = expert_rows()
        y_ref[...] = jnp.where(mine, y, 0.0)

    @pl.when(jnp.logical_and(hi > lo, first_ref[v] == 0))
    def _():
        y, mine = expert_rows()
        y_ref[...] = jnp.where(mine, y, y_ref[...])


def _moe_visits(counts):
    ends = jnp.cumsum(counts)
    starts = ends - counts
    first_tile = starts // MOE_TM
    last_tile = (ends - 1) // MOE_TM
    ntl = jnp.where(counts > 0, last_tile - first_tile + 1, 0)
    vend = jnp.cumsum(ntl)
    vstart = vend - ntl
    total = vend[-1]
    v = jnp.arange(MOE_VISITS, dtype=I32)
    vc = jnp.minimum(v, total - 1)
    grp = jnp.minimum(jnp.sum((vend[None, :] <= vc[:, None]).astype(I32), axis=1), N_EXPERTS - 1)
    tile = first_tile[grp] + (vc - vstart[grp])
    valid = v < total
    lo = jnp.where(valid, jnp.clip(starts[grp] - tile * MOE_TM, 0, MOE_TM), 0)
    hi = jnp.where(valid, jnp.clip(ends[grp] - tile * MOE_TM, 0, MOE_TM), 0)
    prev_tile = jnp.concatenate([jnp.full((1,), -1, I32), tile[:-1]])
    first = jnp.logical_and(valid, tile != prev_tile).astype(I32)
    return tile.astype(I32), grp, lo.astype(I32), hi.astype(I32), first, starts


def _grouped_mlp(visits, xs, layer, w_up, b_up, w_down, b_down):
    tile, grp, lo, hi, first = visits
    depth = w_up.shape[0]
    return pl.pallas_call(
        _gmm_kernel,
        out_shape=jax.ShapeDtypeStruct((MOE_ROWS, D), F32),
        grid_spec=pltpu.PrefetchScalarGridSpec(
            num_scalar_prefetch=5,
            grid=(MOE_VISITS,),
            in_specs=[pl.BlockSpec((MOE_TM, D), lambda v, vt, vg, lo, hi, fi: (vt[v], 0)),
                      pl.BlockSpec((1, 1, D, 2 * D_FF), lambda v, vt, vg, lo, hi, fi: (layer, vg[v], 0, 0)),
                      pl.BlockSpec((1, 1, 1, 2 * D_FF), lambda v, vt, vg, lo, hi, fi: (layer, vg[v], 0, 0)),
                      pl.BlockSpec((1, 1, D_FF, D), lambda v, vt, vg, lo, hi, fi: (layer, vg[v], 0, 0)),
                      pl.BlockSpec((1, 1, 1, D), lambda v, vt, vg, lo, hi, fi: (layer, vg[v], 0, 0))],
            out_specs=pl.BlockSpec((MOE_TM, D), lambda v, vt, vg, lo, hi, fi: (vt[v], 0)),
            scratch_shapes=[pltpu.VMEM((D, 2 * D_FF), BF16), pltpu.VMEM((D_FF, D), BF16)]),
        compiler_params=_cparams(1),
        name="moe_grouped_mlp",
    )(tile, grp, lo, hi, first, xs, w_up, b_up.reshape(depth, N_EXPERTS, 1, 2 * D_FF), w_down,
      b_down.reshape(depth, N_EXPERTS, 1, D))


def _combine_kernel(dest_ref, ys_ref, rw_ref, x_ref, m_ref, g_ref, o_ref, buf, sem):
    base = pl.program_id(0) * (TC * TOP_K)

    def row_copy(i, k):
        d = dest_ref[base + i * TOP_K + k]
        return pltpu.make_async_copy(ys_ref.at[pl.ds(d, 1)], buf.at[k, pl.ds(i, 1)], sem)

    def start(i, c):
        for k in range(TOP_K):
            row_copy(i, k).start(priority=k % 2)
        return c

    def wait(i, c):
        for k in range(TOP_K):
            row_copy(i, k).wait()
        return c

    lax.fori_loop(0, TC, start, 0, unroll=DMA_UNROLL)
    lax.fori_loop(0, TC, wait, 0, unroll=DMA_UNROLL)
    w = rw_ref[...]
    f = w[:, 0:1] * buf[0]
    for k in range(1, TOP_K):
        f = f + w[:, k:k + 1] * buf[k]
    o_ref[...] = x_ref[...] + m_ref[0, 5:6, :] * _rms(f, g_ref[...])


def _combine(dest, ys, rw, x, mod, g_post):
    return pl.pallas_call(
        _combine_kernel,
        out_shape=jax.ShapeDtypeStruct((T_ALL, D), F32),
        grid_spec=pltpu.PrefetchScalarGridSpec(
            num_scalar_prefetch=1,
            grid=(T_ALL // TC,),
            in_specs=[pl.BlockSpec(memory_space=pl.ANY),
                      pl.BlockSpec((TC, LANES), lambda i, dest: (i, 0)),
                      pl.BlockSpec((TC, D), lambda i, dest: (i, 0)),
                      pl.BlockSpec((1, 6, D), lambda i, dest: (_mod_idx(i, TC), 0, 0)),
                      pl.BlockSpec((1, D), lambda i, dest: (0, 0))],
            out_specs=pl.BlockSpec((TC, D), lambda i, dest: (i, 0)),
            scratch_shapes=[pltpu.VMEM((TOP_K, TC, D), F32), pltpu.SemaphoreType.DMA(())]),
        compiler_params=_cparams(1),
        name="moe_combine",
    )(dest, ys, rw, x, mod, g_post.reshape(1, D))


def _moe(x, mod, g_pre, g_post, w_router, b_router, layer, w_up, b_up, w_down, b_down):
    h, ri, rw, cnt = _router(x, g_pre, mod, w_router, b_router)
    counts = cnt[0, :N_EXPERTS].astype(I32)
    tile, grp, lo, hi, first, starts = _moe_visits(counts)
    dest = (starts[ri[:, :TOP_K]] + ri[:, TOP_K:2 * TOP_K]).reshape(-1).astype(I32)
    xs = _dispatch(dest, h)
    ys = _grouped_mlp((tile, grp, lo, hi, first), xs, layer, w_up, b_up, w_down, b_down)
    return _combine(dest, ys, rw, x, mod, g_post)


def kernel(x_prompt, x_sample, cache_k, cache_v, state_fwd, state_bwd, c, c_ctx, w_ada, b_ada, g_pre_mix, g_post_mix, g_pre_ffn, g_post_ffn, na_w_qkv, na_rpb, na_w_out, gla_w_in, gla_w_g1_fwd, gla_w_g2_fwd, gla_b_g_fwd, gla_w_g1_bwd, gla_w_g2_bwd, gla_b_g_bwd, gla_norm_g, gla_w_out, moe_w_router, moe_b_router, moe_w_up, moe_b_up, moe_w_down, moe_b_down):
    x = jnp.concatenate([x_prompt.reshape(T_CTX, D), x_sample.reshape(T_LAT, D)], axis=0)
    cond = jnp.concatenate([c_ctx[None, :], c, jnp.zeros((8 - 1 - N_LAT_SEQ, D), F32)], axis=0)
    mods = _modulation(cond, w_ada, b_ada)

    qkv = _norm_mod_matmul(x, g_pre_mix[0], mods[0], na_w_qkv[0], 0, D)
    a_ctx = _ctx_attention(qkv)
    a_lat = _na_attention(qkv, cache_k[:, 0], cache_v[:, 0], na_rpb[0])
    x = _proj_residual(a_ctx, a_lat, x, mods[0], g_post_mix[0], na_w_out[0], 2)
    new_k = qkv[:T_CTX, D:2 * D].reshape(N_CTX_SEQ, 1, CTX_LEN, NA_HEADS, NA_HD)
    new_v = qkv[:T_CTX, 2 * D:].reshape(N_CTX_SEQ, 1, CTX_LEN, NA_HEADS, NA_HD)
    x = _moe(x, mods[0], g_pre_ffn[0], g_post_ffn[0], moe_w_router[0], moe_b_router[0],
             0, moe_w_up, moe_b_up, moe_w_down, moe_b_down)

    proj = _norm_mod_matmul(x, g_pre_mix[1], mods[1], gla_w_in[0], 0, D)
    gates = _gla_gates(x, g_pre_mix[1], mods[1], gla_w_g1_fwd[0], gla_w_g2_fwd[0], gla_b_g_fwd[0],
                       gla_w_g1_bwd[0], gla_w_g2_bwd[0], gla_b_g_bwd[0])
    of_c, ob_c, s_f, s_b = _gla_scan(proj, gates, seq_len=CTX_LEN, n_seq=N_CTX_SEQ, seq_off=0, out_state=True)
    of_l, ob_l = _gla_scan(proj, gates, seq_len=LAT_LEN, n_seq=N_LAT_SEQ, seq_off=T_CTX // LAT_LEN,
                           s_f0=state_fwd[:, 0], s_b0=state_bwd[:, 0], rotary=True)
    x = _gla_output(of_c, ob_c, of_l, ob_l, proj, gla_norm_g[0], x, mods[1], g_post_mix[1], gla_w_out[0])
    x = _moe(x, mods[1], g_pre_ffn[1], g_post_ffn[1], moe_w_router[1], moe_b_router[1],
             1, moe_w_up, moe_b_up, moe_w_down, moe_b_down)

    return (x[:T_CTX].reshape(N_CTX_SEQ, CTX_LEN, D), x[T_CTX:].reshape(N_LAT_SEQ, LAT_LEN, D),
            new_k, new_v, s_f[:, None], s_b[:, None])
```

```python
import functools

import numpy as np
import jax
import jax.numpy as jnp
from jax import lax
from jax.experimental import pallas as pl
from jax.experimental.pallas import tpu as pltpu

F32 = jnp.float32
BF16 = jnp.bfloat16
I32 = jnp.int32
HIGHEST = lax.Precision.HIGHEST

D = 1024
N_CTX_SEQ = 32
CTX_LEN = 256
N_LAT_SEQ = 2
LAT_LEN = 4096
T_CTX = N_CTX_SEQ * CTX_LEN
T_LAT = N_LAT_SEQ * LAT_LEN
T_ALL = T_CTX + T_LAT
GRID_W = 64
GRID_ROWS = LAT_LEN // GRID_W
NA_HEADS = 16
NA_HD = 64
NA_WIN_ROWS = 8
NA_WIN_COLS = 16
GLA_HEADS = 4
GLA_DK = 128
GLA_DV = 256
GLA_RANK = 16
GLA_GATE_NORM = 16.0
GLA_CHUNK = 64
ROPE_THETA = 10000.0
N_EXPERTS = 32
TOP_K = 4
D_FF = 1024
SWIGLU_LIMIT = 7.0
SWIGLU_ALPHA = 1.702
EPS = 1e-6
NEG_INF = -1e30

LANES = 128
TM = 512
GLA_BLOCK = 256
MOE_TM = 512
MOE_ROWS = T_ALL * TOP_K
MOE_TILES = MOE_ROWS // MOE_TM
MOE_VISITS = MOE_TILES + N_EXPERTS - 1
TD = 512
TC = 256
DMA_UNROLL = 8
VMEM_LIMIT = 60 * 1024 * 1024

_NT = (((1,), (1,)), ((), ()))
_TN = (((0,), (0,)), ((), ()))


def _cparams(n_axes, vmem=None):
    return pltpu.CompilerParams(
        dimension_semantics=("arbitrary",) * n_axes,
        vmem_limit_bytes=VMEM_LIMIT if vmem is None else vmem)


def _mod_idx(i, tm):
    return jnp.maximum((i * tm) // LAT_LEN - 1, 0)


def _rms(x, g):
    return x * lax.rsqrt(jnp.mean(x * x, axis=-1, keepdims=True) + EPS) * g


def _norm_mod(x, g, m_ref, shift_row):
    sh = m_ref[0, shift_row:shift_row + 1, :]
    sc = m_ref[0, shift_row + 1:shift_row + 2, :]
    return _rms(x, g) * (1.0 + sc) + sh


def _mod_kernel(c_ref, w_ref, b_ref, o_ref):
    c = c_ref[...]
    s = c * jax.nn.sigmoid(c)
    o_ref[0] = jnp.dot(s, w_ref[0], precision=HIGHEST, preferred_element_type=F32) + b_ref[0]


def _modulation(cond, w_ada, b_ada):
    depth = w_ada.shape[0]
    out = pl.pallas_call(
        _mod_kernel,
        out_shape=jax.ShapeDtypeStruct((depth, 8, 6 * D), F32),
        grid=(depth, 6),
        in_specs=[pl.BlockSpec((8, D), lambda l, j: (0, 0)),
                  pl.BlockSpec((1, D, D), lambda l, j: (l, 0, j)),
                  pl.BlockSpec((1, 1, D), lambda l, j: (l, 0, j))],
        out_specs=pl.BlockSpec((1, 8, D), lambda l, j: (l, 0, j)),
        compiler_params=_cparams(2),
        name="adaln_modulation",
    )(cond, w_ada, b_ada.reshape(depth, 1, 6 * D))
    return out.reshape(depth, 8, 6, D)


def _nmm_kernel(x_ref, g_ref, m_ref, w_ref, o_ref, wb_ref, *, shift_row):
    @pl.when(pl.program_id(0) == 0)
    def _():
        wb_ref[...] = w_ref[...].astype(BF16)

    h = _norm_mod(x_ref[...], g_ref[...], m_ref, shift_row)
    o_ref[...] = jnp.dot(h.astype(BF16), wb_ref[...], preferred_element_type=F32)


def _norm_mod_matmul(x, g, mod, w, shift_row):
    t, n = x.shape[0], w.shape[1]
    return pl.pallas_call(
        functools.partial(_nmm_kernel, shift_row=shift_row),
        out_shape=jax.ShapeDtypeStruct((t, n), F32),
        grid=(t // TM,),
        in_specs=[pl.BlockSpec((TM, D), lambda i: (i, 0)),
                  pl.BlockSpec((1, D), lambda i: (0, 0)),
                  pl.BlockSpec((1, 6, D), lambda i: (_mod_idx(i, TM), 0, 0)),
                  pl.BlockSpec((D, n), lambda i: (0, 0), pipeline_mode=pl.Buffered(1))],
        out_specs=pl.BlockSpec((TM, n), lambda i: (i, 0)),
        scratch_shapes=[pltpu.VMEM((D, n), BF16)],
        compiler_params=_cparams(1),
        name="norm_mod_matmul",
    )(x, g.reshape(1, D), mod, w)


def _ctx_attn_kernel(qkv_ref, o_ref):
    lane = lax.broadcasted_iota(I32, (1, LANES), 1)
    scale = NA_HD ** -0.5
    for hp in range(NA_HEADS // 2):
        q = qkv_ref[0, :, hp * LANES:(hp + 1) * LANES] * scale
        k = qkv_ref[0, :, D + hp * LANES:D + (hp + 1) * LANES].astype(BF16)
        v = qkv_ref[0, :, 2 * D + hp * LANES:2 * D + (hp + 1) * LANES]
        acc = jnp.zeros((CTX_LEN, LANES), F32)
        for half in range(2):
            msk = (lane < NA_HD) if half == 0 else (lane >= NA_HD)
            qm = jnp.where(msk, q, 0.0).astype(BF16)
            s = lax.dot_general(qm, k, _NT, preferred_element_type=F32)
            p = jnp.exp(s - jnp.max(s, axis=-1, keepdims=True))
            l = jnp.sum(p, axis=-1, keepdims=True)
            vm = jnp.where(msk, v, 0.0).astype(BF16)
            acc = acc + jnp.dot(p.astype(BF16), vm, preferred_element_type=F32) / l
        o_ref[0, :, hp * LANES:(hp + 1) * LANES] = acc


def _ctx_attention(qkv):
    out = pl.pallas_call(
        _ctx_attn_kernel,
        out_shape=jax.ShapeDtypeStruct((N_CTX_SEQ, CTX_LEN, D), F32),
        grid=(N_CTX_SEQ,),
        in_specs=[pl.BlockSpec((1, CTX_LEN, 3 * D), lambda b: (b, 0, 0))],
        out_specs=pl.BlockSpec((1, CTX_LEN, D), lambda b: (b, 0, 0)),
        compiler_params=_cparams(1),
        name="context_attention",
    )(qkv.reshape(T_ALL // CTX_LEN, CTX_LEN, 3 * D))
    return out.reshape(T_CTX, D)


NA_QR = 8
NA_KR = 2 * NA_QR
NA_NBLK = GRID_ROWS // NA_QR
NA_N_OFF = 2 * NA_WIN_ROWS - 1
NA_SLAB_BOTH, NA_SLAB_LEFT, NA_SLAB_RIGHT, NA_SLAB_NONE = 0, NA_N_OFF - 1, 2 * NA_N_OFF - 1, 3 * NA_N_OFF - 1
NA_NSLAB = 3 * NA_N_OFF


def _na_slab_ids():
    ids = np.zeros((3, NA_QR, NA_KR // 2), np.int64)
    half = NA_WIN_ROWS // 2
    starts = ((0, 0), (NA_QR, NA_QR - half), (GRID_ROWS - NA_QR, GRID_ROWS - NA_KR))
    for v, (rb, kb) in enumerate(starts):
        for qr in range(NA_QR):
            qa = rb + qr
            r0 = min(max(qa - half, 0), GRID_ROWS - NA_WIN_ROWS)
            for m in range(NA_KR // 2):
                kl, kr = kb + 2 * m, kb + 2 * m + 1
                vl, vr = r0 <= kl < r0 + NA_WIN_ROWS, r0 <= kr < r0 + NA_WIN_ROWS
                ol, orr = kl - qa + NA_WIN_ROWS - 1, kr - qa + NA_WIN_ROWS - 1
                if vl and vr:
                    ids[v, qr, m] = NA_SLAB_BOTH + ol
                elif vl:
                    ids[v, qr, m] = NA_SLAB_LEFT + ol
                elif vr:
                    ids[v, qr, m] = NA_SLAB_RIGHT + orr
                else:
                    ids[v, qr, m] = NA_SLAB_NONE
    return ids


def _na_kernel(q_ref, k_ref, v_ref, ck_ref, cv_ref, slab_ref, o_ref, bias_scr):
    slab_ids = _na_slab_ids()
    for half in range(2):
        for v in range(3):
            for qr in range(NA_QR):
                for m in range(NA_KR // 2):
                    bias_scr[half, v, qr * GRID_W:(qr + 1) * GRID_W, m * LANES:(m + 1) * LANES] = (
                        slab_ref[half, int(slab_ids[v, qr, m])])

    lane = lax.broadcasted_iota(I32, (1, LANES), 1)
    masks = ((lane < NA_HD), (lane >= NA_HD))
    scale = NA_HD ** -0.5
    ck = ck_ref[0].astype(BF16)
    cv = cv_ref[0]
    cvm = [jnp.where(m, cv, 0.0).astype(BF16) for m in masks]
    nq = NA_QR * GRID_W
    nk = NA_KR * GRID_W

    def body(blk, carry):
        rb = blk * NA_QR
        kb = jnp.clip(rb - NA_WIN_ROWS // 2, 0, GRID_ROWS - NA_KR)
        layout = jnp.where(blk == 0, 0, jnp.where(blk == NA_NBLK - 1, 2, 1))
        qs = pl.ds(pl.multiple_of(rb * GRID_W, nq), nq)
        ws = pl.ds(pl.multiple_of(kb * GRID_W, GRID_W * NA_WIN_ROWS // 2), nk)
        q = q_ref[0, qs, :] * scale
        kw = k_ref[0, ws, :].astype(BF16)
        vw = v_ref[0, ws, :]
        acc = jnp.zeros((nq, LANES), F32)
        for half in range(2):
            qm = jnp.where(masks[half], q, 0.0).astype(BF16)
            sw = lax.dot_general(qm, kw, _NT, preferred_element_type=F32) + bias_scr[half, pl.ds(layout, 1)][0]
            sc = lax.dot_general(qm, ck, _NT, preferred_element_type=F32)
            m = jnp.maximum(jnp.max(sw, axis=-1, keepdims=True), jnp.max(sc, axis=-1, keepdims=True))
            pw = jnp.exp(sw - m)
            pc = jnp.exp(sc - m)
            l = jnp.sum(pw, axis=-1, keepdims=True) + jnp.sum(pc, axis=-1, keepdims=True)
            vm = jnp.where(masks[half], vw, 0.0).astype(BF16)
            o = (jnp.dot(pw.astype(BF16), vm, preferred_element_type=F32)
                 + jnp.dot(pc.astype(BF16), cvm[half], preferred_element_type=F32))
            acc = acc + o / l
        o_ref[0, qs, :] = acc
        return carry

    lax.fori_loop(0, NA_NBLK, body, 0)


def _na_slab_table(rpb):
    cidx = np.arange(GRID_W)
    col_start = np.clip(cidx - NA_WIN_COLS // 2, 0, GRID_W - NA_WIN_COLS)
    col_ok = (cidx[None, :] >= col_start[:, None]) & (cidx[None, :] < col_start[:, None] + NA_WIN_COLS)
    coff = np.clip(cidx[None, :] - cidx[:, None], -(NA_WIN_COLS - 1), NA_WIN_COLS - 1) + NA_WIN_COLS - 1
    n_coff = 2 * NA_WIN_COLS - 1
    onehot = (coff[None, :, :] == np.arange(n_coff)[:, None, None]).astype(np.float32)
    tab = jnp.einsum("hrj,jcw->hrcw", rpb.astype(F32), onehot, precision=HIGHEST)
    tab = jnp.where(col_ok[None, None], tab, NEG_INF)
    neg = jnp.full_like(tab, NEG_INF)
    both = jnp.concatenate([tab[:, :-1], tab[:, 1:]], axis=-1)
    left = jnp.concatenate([tab, neg], axis=-1)
    right = jnp.concatenate([neg, tab], axis=-1)
    none = jnp.concatenate([neg[:, :1], neg[:, :1]], axis=-1)
    return jnp.concatenate([both, left, right, none], axis=1)


def _na_attention(qkv, cache_k, cache_v, rpb):
    off = T_CTX // LAT_LEN
    qkv3 = qkv.reshape(T_ALL // LAT_LEN, LAT_LEN, 3 * D)
    ck = cache_k.reshape(N_LAT_SEQ, CTX_LEN, D)
    cv = cache_v.reshape(N_LAT_SEQ, CTX_LEN, D)
    slabs = _na_slab_table(rpb)
    nh = D // LANES
    out = pl.pallas_call(
        _na_kernel,
        out_shape=jax.ShapeDtypeStruct((N_LAT_SEQ, LAT_LEN, D), F32),
        grid=(N_LAT_SEQ, nh),
        in_specs=[pl.BlockSpec((1, LAT_LEN, LANES), lambda b, h: (b + off, 0, h)),
                  pl.BlockSpec((1, LAT_LEN, LANES), lambda b, h: (b + off, 0, nh + h)),
                  pl.BlockSpec((1, LAT_LEN, LANES), lambda b, h: (b + off, 0, 2 * nh + h)),
                  pl.BlockSpec((1, CTX_LEN, LANES), lambda b, h: (b, 0, h)),
                  pl.BlockSpec((1, CTX_LEN, LANES), lambda b, h: (b, 0, h)),
                  pl.BlockSpec((2, NA_NSLAB, GRID_W, 2 * GRID_W), lambda b, h: (h, 0, 0, 0))],
        out_specs=pl.BlockSpec((1, LAT_LEN, LANES), lambda b, h: (b, 0, h)),
        scratch_shapes=[pltpu.VMEM((2, 3, NA_QR * GRID_W, NA_KR * GRID_W), F32)],
        compiler_params=_cparams(2),
        name="neighbourhood_attention",
    )(qkv3, qkv3, qkv3, ck, cv, slabs)
    return out.reshape(T_LAT, D)


CTX_TILES = T_CTX // TM


def _ctx_part(i):
    return (jnp.minimum(i, CTX_TILES - 1), 0)


def _lat_part(i):
    return (jnp.maximum(i - CTX_TILES, 0), 0)


def _proj_res_kernel(ac_ref, al_ref, x_ref, m_ref, g_ref, w_ref, o_ref, wb_ref, *, gate_row):
    i = pl.program_id(0)

    @pl.when(i == 0)
    def _():
        wb_ref[...] = w_ref[...].astype(BF16)

    a = jnp.where(i < CTX_TILES, ac_ref[...], al_ref[...])
    y = jnp.dot(a.astype(BF16), wb_ref[...], preferred_element_type=F32)
    o_ref[...] = x_ref[...] + m_ref[0, gate_row:gate_row + 1, :] * _rms(y, g_ref[...])


def _proj_residual(a_ctx, a_lat, x, mod, g_post, w, gate_row):
    return pl.pallas_call(
        functools.partial(_proj_res_kernel, gate_row=gate_row),
        out_shape=jax.ShapeDtypeStruct((T_ALL, D), F32),
        grid=(T_ALL // TM,),
        in_specs=[pl.BlockSpec((TM, D), _ctx_part),
                  pl.BlockSpec((TM, D), _lat_part),
                  pl.BlockSpec((TM, D), lambda i: (i, 0)),
                  pl.BlockSpec((1, 6, D), lambda i: (_mod_idx(i, TM), 0, 0)),
                  pl.BlockSpec((1, D), lambda i: (0, 0)),
                  pl.BlockSpec((D, D), lambda i: (0, 0))],
        out_specs=pl.BlockSpec((TM, D), lambda i: (i, 0)),
        scratch_shapes=[pltpu.VMEM((D, D), BF16)],
        compiler_params=_cparams(1),
        name="proj_residual",
    )(a_ctx, a_lat, x, mod, g_post.reshape(1, D), w)


def _gla_gate_kernel(x_ref, g_ref, m_ref, w1_ref, w2_ref, b_ref, o_ref):
    h = _norm_mod(x_ref[...], g_ref[...], m_ref, 0)
    z = jnp.dot(h.astype(BF16), w1_ref[...].astype(BF16), preferred_element_type=F32)
    y = jnp.dot(z, w2_ref[...], precision=HIGHEST, preferred_element_type=F32) + b_ref[...]
    o_ref[...] = (jnp.minimum(y, 0.0) - jnp.log(1.0 + jnp.exp(-jnp.abs(y)))) * (1.0 / GLA_GATE_NORM)


def _gla_gates(x, g, mod, w1f, w2f, bf, w1b, w2b, bb):
    hk = GLA_HEADS * GLA_DK
    w1 = jnp.zeros((D, LANES), F32).at[:, :GLA_RANK].set(w1f).at[:, GLA_RANK:2 * GLA_RANK].set(w1b)
    w2 = jnp.zeros((LANES, 2 * hk), F32).at[:GLA_RANK, :hk].set(w2f).at[GLA_RANK:2 * GLA_RANK, hk:].set(w2b)
    b = jnp.concatenate([bf, bb]).reshape(1, 2 * hk)
    return pl.pallas_call(
        _gla_gate_kernel,
        out_shape=jax.ShapeDtypeStruct((T_ALL, 2 * hk), F32),
        grid=(T_ALL // TM,),
        in_specs=[pl.BlockSpec((TM, D), lambda i: (i, 0)),
                  pl.BlockSpec((1, D), lambda i: (0, 0)),
                  pl.BlockSpec((1, 6, D), lambda i: (_mod_idx(i, TM), 0, 0)),
                  pl.BlockSpec((D, LANES), lambda i: (0, 0)),
                  pl.BlockSpec((LANES, 2 * hk), lambda i: (0, 0)),
                  pl.BlockSpec((1, 2 * hk), lambda i: (0, 0))],
        out_specs=pl.BlockSpec((TM, 2 * hk), lambda i: (i, 0)),
        compiler_params=_cparams(1),
        name="gla_gates",
    )(x, g.reshape(1, D), mod, w1, w2, b)


def _rope(x, cos, sin_signed):
    lane = lax.broadcasted_iota(I32, (1, LANES), 1)
    partner = jnp.where((lane % 64) < 32, pltpu.roll(x, LANES - 32, 1), pltpu.roll(x, 32, 1))
    return x * cos + partner * sin_signed


def _gla_chunk(q, k, v, g, st, tri, forward):
    L = GLA_CHUNK
    cum = jnp.dot(tri.astype(F32), g, precision=HIGHEST, preferred_element_type=F32)
    cl = cum[L - 1:L, :] if forward else cum[0:1, :]
    q_dec = (q * (GLA_DK ** -0.5) * jnp.exp(cum)).astype(BF16)
    k_dec = (k * jnp.exp(-cum)).astype(BF16)
    k_rem = (k * jnp.exp(cl - cum)).astype(BF16)
    vb = v.astype(BF16)
    a = lax.dot_general(q_dec, k_dec, _NT, preferred_element_type=F32)
    a = jnp.where(tri, a, 0.0).astype(BF16)
    o = (jnp.dot(a, vb, preferred_element_type=F32)
         + lax.dot_general(q_dec, st.astype(BF16), _NT, preferred_element_type=F32))
    kv_t = lax.dot_general(vb, k_rem, _TN, preferred_element_type=F32)
    return o, jnp.exp(cl) * st + kv_t


def _gla_kernel(*refs, rotary, has_init, out_state, nblk):
    refs = list(refs)
    qf, kf, vf, gf, qb, kb, vb, gb = refs[:8]
    del refs[:8]
    if rotary:
        cosf, sinf, cosb, sinb = refs[:4]
        del refs[:4]
    if has_init:
        sf0, sb0 = refs[:2]
        del refs[:2]
    of, ob = refs[:2]
    del refs[:2]
    if out_state:
        sfo, sbo = refs[:2]
        del refs[:2]
    st_f, st_b = refs

    j = pl.program_id(2)
    L = GLA_CHUNK

    @pl.when(j == 0)
    def _():
        if has_init:
            st_f[...] = sf0[0, 0].T
            st_b[...] = sb0[0, 0].T
        else:
            st_f[...] = jnp.zeros_like(st_f)
            st_b[...] = jnp.zeros_like(st_b)

    row = lax.broadcasted_iota(I32, (L, L), 0)
    col = lax.broadcasted_iota(I32, (L, L), 1)
    tri_f = col <= row
    tri_b = col >= row

    nchunk = GLA_BLOCK // L
    s = st_f[...]
    for c in range(nchunk):
        sl = slice(c * L, (c + 1) * L)
        q, k = qf[0, sl, :], kf[0, sl, :]
        if rotary:
            q = _rope(q, cosf[sl, :], sinf[sl, :])
            k = _rope(k, cosf[sl, :], sinf[sl, :])
        o, s = _gla_chunk(q, k, vf[0, sl, :], gf[0, sl, :], s, tri_f, True)
        of[0, sl, :] = o
    st_f[...] = s

    s = st_b[...]
    for c in reversed(range(nchunk)):
        sl = slice(c * L, (c + 1) * L)
        q, k = qb[0, sl, :], kb[0, sl, :]
        if rotary:
            q = _rope(q, cosb[sl, :], sinb[sl, :])
            k = _rope(k, cosb[sl, :], sinb[sl, :])
        o, s = _gla_chunk(q, k, vb[0, sl, :], gb[0, sl, :], s, tri_b, False)
        ob[0, sl, :] = o
    st_b[...] = s

    if out_state:
        @pl.when(j == nblk - 1)
        def _():
            sfo[0, 0] = st_f[...].T
            sbo[0, 0] = st_b[...].T


def _rope_tables(n):
    t = np.arange(n)
    n_freq = GLA_DK // 4
    inv = ROPE_THETA ** (-np.arange(n_freq, dtype=np.float64) / n_freq)
    ang_r = (t // GRID_W).astype(np.float64)[:, None] * inv[None, :]
    ang_c = (t % GRID_W).astype(np.float64)[:, None] * inv[None, :]
    cos = np.concatenate([np.cos(ang_r), np.cos(ang_r), np.cos(ang_c), np.cos(ang_c)], axis=1)
    sin = np.concatenate([-np.sin(ang_r), np.sin(ang_r), -np.sin(ang_c), np.sin(ang_c)], axis=1)
    return jnp.asarray(cos, F32), jnp.asarray(sin, F32)


def _gla_scan(proj, gates, *, seq_len, n_seq, seq_off, s_f0=None, s_b0=None, rotary=False, out_state=False):
    nseq_all = T_ALL // seq_len
    nblk = seq_len // GLA_BLOCK
    nh = GLA_HEADS
    proj3 = proj.reshape(nseq_all, seq_len, 3 * D)
    g3 = gates.reshape(nseq_all, seq_len, 2 * nh * GLA_DK)
    has_init = s_f0 is not None

    def fwd(c0):
        return lambda b, h, j: (b + seq_off, j, c0 + h)

    def bwd(c0):
        return lambda b, h, j: (b + seq_off, nblk - 1 - j, c0 + h)

    qk = (1, GLA_BLOCK, GLA_DK)
    vv = (1, GLA_BLOCK, GLA_DV)
    v_c0 = 2 * nh * GLA_DK // GLA_DV
    in_specs = [pl.BlockSpec(qk, fwd(0)), pl.BlockSpec(qk, fwd(nh)), pl.BlockSpec(vv, fwd(v_c0)), pl.BlockSpec(qk, fwd(0)),
                pl.BlockSpec(qk, bwd(0)), pl.BlockSpec(qk, bwd(nh)), pl.BlockSpec(vv, bwd(v_c0)), pl.BlockSpec(qk, bwd(nh))]
    args = [proj3, proj3, proj3, g3, proj3, proj3, proj3, g3]
    if rotary:
        cos, sin = _rope_tables(seq_len)
        tab = (GLA_BLOCK, GLA_DK)
        in_specs += [pl.BlockSpec(tab, lambda b, h, j: (j, 0)), pl.BlockSpec(tab, lambda b, h, j: (j, 0)),
                     pl.BlockSpec(tab, lambda b, h, j: (nblk - 1 - j, 0)), pl.BlockSpec(tab, lambda b, h, j: (nblk - 1 - j, 0))]
        args += [cos, sin, cos, sin]
    if has_init:
        st = (1, 1, GLA_DK, GLA_DV)
        in_specs += [pl.BlockSpec(st, lambda b, h, j: (b, h, 0, 0))] * 2
        args += [s_f0, s_b0]
    out_shape = [jax.ShapeDtypeStruct((n_seq, seq_len, D), F32)] * 2
    out_specs = [pl.BlockSpec(vv, lambda b, h, j: (b, j, h)),
                 pl.BlockSpec(vv, lambda b, h, j: (b, nblk - 1 - j, h))]
    if out_state:
        out_shape += [jax.ShapeDtypeStruct((n_seq, nh, GLA_DK, GLA_DV), F32)] * 2
        out_specs += [pl.BlockSpec((1, 1, GLA_DK, GLA_DV), lambda b, h, j: (b, h, 0, 0))] * 2
    res = pl.pallas_call(
        functools.partial(_gla_kernel, rotary=rotary, has_init=has_init, out_state=out_state, nblk=nblk),
        out_shape=out_shape,
        grid=(n_seq, nh, nblk),
        in_specs=in_specs,
        out_specs=out_specs,
        scratch_shapes=[pltpu.VMEM((GLA_DV, GLA_DK), F32)] * 2,
        compiler_params=_cparams(3),
        name="gla_scan_rope" if rotary else "gla_scan",
    )(*args)
    of, ob = res[0].reshape(n_seq * seq_len, D), res[1].reshape(n_seq * seq_len, D)
    if out_state:
        return of, ob, res[2], res[3]
    return of, ob


def _gla_out_kernel(ofc_ref, obc_ref, ofl_ref, obl_ref, r_ref, ng_ref, x_ref, m_ref, g_ref, w_ref, o_ref, wb_ref):
    i = pl.program_id(0)

    @pl.when(i == 0)
    def _():
        wb_ref[...] = w_ref[...].astype(BF16)

    o = jnp.where(i < CTX_TILES, ofc_ref[...] + obc_ref[...], ofl_ref[...] + obl_ref[...])
    r = r_ref[...]
    ng = ng_ref[...]
    parts = []
    for h in range(GLA_HEADS):
        oh = o[:, h * GLA_DV:(h + 1) * GLA_DV]
        parts.append(_rms(oh, ng))
    a = jnp.concatenate(parts, axis=1) * (r * jax.nn.sigmoid(r))
    y = jnp.dot(a.astype(BF16), wb_ref[...], preferred_element_type=F32)
    o_ref[...] = x_ref[...] + m_ref[0, 2:3, :] * _rms(y, g_ref[...])


def _gla_output(of_ctx, ob_ctx, of_lat, ob_lat, proj, norm_g, x, mod, g_post, w):
    return pl.pallas_call(
        _gla_out_kernel,
        out_shape=jax.ShapeDtypeStruct((T_ALL, D), F32),
        grid=(T_ALL // TM,),
        in_specs=[pl.BlockSpec((TM, D), _ctx_part),
                  pl.BlockSpec((TM, D), _ctx_part),
                  pl.BlockSpec((TM, D), _lat_part),
                  pl.BlockSpec((TM, D), _lat_part),
                  pl.BlockSpec((TM, D), lambda i: (i, 2)),
                  pl.BlockSpec((1, GLA_DV), lambda i: (0, 0)),
                  pl.BlockSpec((TM, D), lambda i: (i, 0)),
                  pl.BlockSpec((1, 6, D), lambda i: (_mod_idx(i, TM), 0, 0)),
                  pl.BlockSpec((1, D), lambda i: (0, 0)),
                  pl.BlockSpec((D, D), lambda i: (0, 0))],
        out_specs=pl.BlockSpec((TM, D), lambda i: (i, 0)),
        scratch_shapes=[pltpu.VMEM((D, D), BF16)],
        compiler_params=_cparams(1),
        name="gla_output",
    )(of_ctx, ob_ctx, of_lat, ob_lat, proj, norm_g.reshape(1, GLA_DV), x, mod, g_post.reshape(1, D), w)


def _router_kernel(x_ref, g_ref, m_ref, wr_ref, br_ref, h_ref, ri_ref, rw_ref, cnt_ref, cnt_scr):
    @pl.when(pl.program_id(0) == 0)
    def _():
        cnt_scr[...] = jnp.zeros_like(cnt_scr)

    h = _norm_mod(x_ref[...], g_ref[...], m_ref, 3)
    h_ref[...] = h
    logits = jnp.dot(h, wr_ref[...], precision=HIGHEST, preferred_element_type=F32) + br_ref[...]
    lane = lax.broadcasted_iota(I32, (TM, LANES), 1)
    lane_f = lane.astype(F32)
    cur = jnp.where(lane < N_EXPERTS, logits, -jnp.inf)
    vals, sels = [], []
    hot = jnp.zeros((TM, LANES), F32)
    for _ in range(TOP_K):
        m = jnp.max(cur, axis=-1, keepdims=True)
        idx = jnp.min(jnp.where(cur == m, lane_f, float(LANES)), axis=-1, keepdims=True)
        sel = lane_f == idx
        vals.append(m)
        sels.append((idx, sel))
        hot = hot + sel.astype(F32)
        cur = jnp.where(sel, -jnp.inf, cur)
    ex = [jnp.exp(v - vals[0]) for v in vals]
    den = ex[0] + ex[1] + ex[2] + ex[3]
    r_i = lax.broadcasted_iota(I32, (TM, TM), 0)
    c_i = lax.broadcasted_iota(I32, (TM, TM), 1)
    before = (c_i < r_i).astype(BF16)
    prefix = jnp.dot(before, hot.astype(BF16), preferred_element_type=F32) + cnt_scr[0:1, :]
    ri = jnp.zeros((TM, LANES), F32)
    rw = jnp.zeros((TM, LANES), F32)
    for k in range(TOP_K):
        idx, sel = sels[k]
        rank = jnp.sum(jnp.where(sel, prefix, 0.0), axis=-1, keepdims=True)
        ri = jnp.where(lane == k, idx, ri)
        ri = jnp.where(lane == TOP_K + k, rank, ri)
        rw = jnp.where(lane == k, ex[k] / den, rw)
    ri_ref[...] = ri.astype(I32)
    rw_ref[...] = rw
    cnt = cnt_scr[...] + jnp.sum(hot, axis=0, keepdims=True)
    cnt_scr[...] = cnt
    cnt_ref[...] = cnt


def _router(x, g, mod, w_router, b_router):
    wr = jnp.zeros((D, LANES), F32).at[:, :N_EXPERTS].set(w_router)
    br = jnp.zeros((1, LANES), F32).at[0, :N_EXPERTS].set(b_router)
    return pl.pallas_call(
        _router_kernel,
        out_shape=[jax.ShapeDtypeStruct((T_ALL, D), F32),
                   jax.ShapeDtypeStruct((T_ALL, LANES), I32),
                   jax.ShapeDtypeStruct((T_ALL, LANES), F32),
                   jax.ShapeDtypeStruct((8, LANES), F32)],
        grid=(T_ALL // TM,),
        in_specs=[pl.BlockSpec((TM, D), lambda i: (i, 0)),
                  pl.BlockSpec((1, D), lambda i: (0, 0)),
                  pl.BlockSpec((1, 6, D), lambda i: (_mod_idx(i, TM), 0, 0)),
                  pl.BlockSpec((D, LANES), lambda i: (0, 0)),
                  pl.BlockSpec((1, LANES), lambda i: (0, 0))],
        out_specs=[pl.BlockSpec((TM, D), lambda i: (i, 0)),
                   pl.BlockSpec((TM, LANES), lambda i: (i, 0)),
                   pl.BlockSpec((TM, LANES), lambda i: (i, 0)),
                   pl.BlockSpec((8, LANES), lambda i: (0, 0))],
        scratch_shapes=[pltpu.VMEM((8, LANES), F32)],
        compiler_params=_cparams(1),
        name="moe_router",
    )(x, g.reshape(1, D), mod, wr, br)


def _dispatch_kernel(dest_ref, h_ref, xs_ref, sem):
    base = pl.program_id(0) * (TD * TOP_K)

    def row_copy(i, k):
        d = dest_ref[base + i * TOP_K + k]
        return pltpu.make_async_copy(h_ref.at[pl.ds(i, 1)], xs_ref.at[pl.ds(d, 1)], sem)

    def start(i, c):
        for k in range(TOP_K):
            row_copy(i, k).start(priority=k % 2)
        return c

    def wait(i, c):
        for k in range(TOP_K):
            row_copy(i, k).wait()
        return c

    lax.fori_loop(0, TD, start, 0, unroll=DMA_UNROLL)
    lax.fori_loop(0, TD, wait, 0, unroll=DMA_UNROLL)


def _dispatch(dest, h):
    return pl.pallas_call(
        _dispatch_kernel,
        out_shape=jax.ShapeDtypeStruct((MOE_ROWS, D), F32),
        grid_spec=pltpu.PrefetchScalarGridSpec(
            num_scalar_prefetch=1,
            grid=(T_ALL // TD,),
            in_specs=[pl.BlockSpec((TD, D), lambda i, dest: (i, 0))],
            out_specs=pl.BlockSpec(memory_space=pl.ANY),
            scratch_shapes=[pltpu.SemaphoreType.DMA(())]),
        compiler_params=_cparams(1),
        name="moe_dispatch",
    )(dest, h)


def _gmm_kernel(vt_ref, vg_ref, lo_ref, hi_ref, first_ref, x_ref, wu_ref, bu_ref, wd_ref, bd_ref, y_ref, wub, wdb):
    v = pl.program_id(0)
    g = vg_ref[v]
    new_group = jnp.logical_or(v == 0, vg_ref[jnp.maximum(v - 1, 0)] != g)

    @pl.when(new_group)
    def _():
        wub[...] = wu_ref[0, 0].astype(BF16)
        wdb[...] = wd_ref[0, 0].astype(BF16)

    lo = lo_ref[v]
    hi = hi_ref[v]

    def expert_rows():
        u = jnp.dot(x_ref[...].astype(BF16), wub[...], preferred_element_type=F32) + bu_ref[0, 0]
        gate = jnp.minimum(u[:, :D_FF], SWIGLU_LIMIT)
        lin = jnp.clip(u[:, D_FF:], -SWIGLU_LIMIT, SWIGLU_LIMIT)
        act = gate * jax.nn.sigmoid(SWIGLU_ALPHA * gate) * (lin + 1.0)
        y = jnp.dot(act.astype(BF16), wdb[...], preferred_element_type=F32) + bd_ref[0, 0]
        row = lax.broadcasted_iota(I32, (MOE_TM, 1), 0)
        return y, jnp.logical_and(row >= lo, row < hi)

    @pl.when(jnp.logical_and(hi > lo, first_ref[v] == 1))
    def _():
        y, mine = expert_rows()
        y_ref[...] = jnp.where(mine, y, 0.0)

    @pl.when(jnp.logical_and(hi > lo, first_ref[v] == 0))
    def _():
        y, mine = expert_rows()
        y_ref[...] = jnp.where(mine, y, y_ref[...])


def _moe_visits(counts):
    ends = jnp.cumsum(counts)
    starts = ends - counts
    first_tile = starts // MOE_TM
    last_tile = (ends - 1) // MOE_TM
    ntl = jnp.where(counts > 0, last_tile - first_tile + 1, 0)
    vend = jnp.cumsum(ntl)
    vstart = vend - ntl
    total = vend[-1]
    v = jnp.arange(MOE_VISITS, dtype=I32)
    vc = jnp.minimum(v, total - 1)
    grp = jnp.minimum(jnp.sum((vend[None, :] <= vc[:, None]).astype(I32), axis=1), N_EXPERTS - 1)
    tile = first_tile[grp] + (vc - vstart[grp])
    valid = v < total
    lo = jnp.where(valid, jnp.clip(starts[grp] - tile * MOE_TM, 0, MOE_TM), 0)
    hi = jnp.where(valid, jnp.clip(ends[grp] - tile * MOE_TM, 0, MOE_TM), 0)
    prev_tile = jnp.concatenate([jnp.full((1,), -1, I32), tile[:-1]])
    first = jnp.logical_and(valid, tile != prev_tile).astype(I32)
    return tile.astype(I32), grp, lo.astype(I32), hi.astype(I32), first, starts


def _grouped_mlp(visits, xs, layer, w_up, b_up, w_down, b_down):
    tile, grp, lo, hi, first = visits
    depth = w_up.shape[0]
    return pl.pallas_call(
        _gmm_kernel,
        out_shape=jax.ShapeDtypeStruct((MOE_ROWS, D), F32),
        grid_spec=pltpu.PrefetchScalarGridSpec(
            num_scalar_prefetch=5,
            grid=(MOE_VISITS,),
            in_specs=[pl.BlockSpec((MOE_TM, D), lambda v, vt, vg, lo, hi, fi: (vt[v], 0)),
                      pl.BlockSpec((1, 1, D, 2 * D_FF), lambda v, vt, vg, lo, hi, fi: (layer, vg[v], 0, 0)),
                      pl.BlockSpec((1, 1, 1, 2 * D_FF), lambda v, vt, vg, lo, hi, fi: (layer, vg[v], 0, 0)),
                      pl.BlockSpec((1, 1, D_FF, D), lambda v, vt, vg, lo, hi, fi: (layer, vg[v], 0, 0)),
                      pl.BlockSpec((1, 1, 1, D), lambda v, vt, vg, lo, hi, fi: (layer, vg[v], 0, 0))],
            out_specs=pl.BlockSpec((MOE_TM, D), lambda v, vt, vg, lo, hi, fi: (vt[v], 0)),
            scratch_shapes=[pltpu.VMEM((D, 2 * D_FF), BF16), pltpu.VMEM((D_FF, D), BF16)]),
        compiler_params=_cparams(1),
        name="moe_grouped_mlp",
    )(tile, grp, lo, hi, first, xs, w_up, b_up.reshape(depth, N_EXPERTS, 1, 2 * D_FF), w_down,
      b_down.reshape(depth, N_EXPERTS, 1, D))


def _combine_kernel(dest_ref, ys_ref, rw_ref, x_ref, m_ref, g_ref, o_ref, buf, sem):
    step = pl.program_id(0)
    slot = step % 2

    def row_copy(s, i, k):
        d = dest_ref[s * (TC * TOP_K) + i * TOP_K + k]
        return pltpu.make_async_copy(ys_ref.at[pl.ds(d, 1)], buf.at[s % 2, k, pl.ds(i, 1)], sem.at[s % 2])

    def issue(s):
        def start(i, c):
            for k in range(TOP_K):
                row_copy(s, i, k).start(priority=k % 2)
            return c

        lax.fori_loop(0, TC, start, 0, unroll=DMA_UNROLL)

    @pl.when(step == 0)
    def _():
        issue(step)

    @pl.when(step + 1 < pl.num_programs(0))
    def _():
        issue(step + 1)

    def wait(i, c):
        for k in range(TOP_K):
            row_copy(step, i, k).wait()
        return c

    lax.fori_loop(0, TC, wait, 0, unroll=DMA_UNROLL)
    w = rw_ref[...]
    f = w[:, 0:1] * buf[slot, 0]
    for k in range(1, TOP_K):
        f = f + w[:, k:k + 1] * buf[slot, k]
    o_ref[...] = x_ref[...] + m_ref[0, 5:6, :] * _rms(f, g_ref[...])


def _combine(dest, ys, rw, x, mod, g_post):
    return pl.pallas_call(
        _combine_kernel,
        out_shape=jax.ShapeDtypeStruct((T_ALL, D), F32),
        grid_spec=pltpu.PrefetchScalarGridSpec(
            num_scalar_prefetch=1,
            grid=(T_ALL // TC,),
            in_specs=[pl.BlockSpec(memory_space=pl.ANY),
                      pl.BlockSpec((TC, LANES), lambda i, dest: (i, 0)),
                      pl.BlockSpec((TC, D), lambda i, dest: (i, 0)),
                      pl.BlockSpec((1, 6, D), lambda i, dest: (_mod_idx(i, TC), 0, 0)),
                      pl.BlockSpec((1, D), lambda i, dest: (0, 0))],
            out_specs=pl.BlockSpec((TC, D), lambda i, dest: (i, 0)),
            scratch_shapes=[pltpu.VMEM((2, TOP_K, TC, D), F32), pltpu.SemaphoreType.DMA((2,))]),
        compiler_params=_cparams(1),
        name="moe_combine",
    )(dest, ys, rw, x, mod, g_post.reshape(1, D))


def _moe(x, mod, g_pre, g_post, w_router, b_router, layer, w_up, b_up, w_down, b_down):
    h, ri, rw, cnt = _router(x, g_pre, mod, w_router, b_router)
    counts = cnt[0, :N_EXPERTS].astype(I32)
    tile, grp, lo, hi, first, starts = _moe_visits(counts)
    dest = (starts[ri[:, :TOP_K]] + ri[:, TOP_K:2 * TOP_K]).reshape(-1).astype(I32)
    xs = _dispatch(dest, h)
    ys = _grouped_mlp((tile, grp, lo, hi, first), xs, layer, w_up, b_up, w_down, b_down)
    return _combine(dest, ys, rw, x, mod, g_post)


def kernel(x_prompt, x_sample, cache_k, cache_v, state_fwd, state_bwd, c, c_ctx, w_ada, b_ada, g_pre_mix, g_post_mix, g_pre_ffn, g_post_ffn, na_w_qkv, na_rpb, na_w_out, gla_w_in, gla_w_g1_fwd, gla_w_g2_fwd, gla_b_g_fwd, gla_w_g1_bwd, gla_w_g2_bwd, gla_b_g_bwd, gla_norm_g, gla_w_out, moe_w_router, moe_b_router, moe_w_up, moe_b_up, moe_w_down, moe_b_down):
    x = jnp.concatenate([x_prompt.reshape(T_CTX, D), x_sample.reshape(T_LAT, D)], axis=0)
    cond = jnp.concatenate([c_ctx[None, :], c, jnp.zeros((8 - 1 - N_LAT_SEQ, D), F32)], axis=0)
    mods = _modulation(cond, w_ada, b_ada)

    qkv = _norm_mod_matmul(x, g_pre_mix[0], mods[0], na_w_qkv[0], 0)
    a_ctx = _ctx_attention(qkv)
    a_lat = _na_attention(qkv, cache_k[:, 0], cache_v[:, 0], na_rpb[0])
    x = _proj_residual(a_ctx, a_lat, x, mods[0], g_post_mix[0], na_w_out[0], 2)
    new_k = qkv[:T_CTX, D:2 * D].reshape(N_CTX_SEQ, 1, CTX_LEN, NA_HEADS, NA_HD)
    new_v = qkv[:T_CTX, 2 * D:].reshape(N_CTX_SEQ, 1, CTX_LEN, NA_HEADS, NA_HD)
    x = _moe(x, mods[0], g_pre_ffn[0], g_post_ffn[0], moe_w_router[0], moe_b_router[0],
             0, moe_w_up, moe_b_up, moe_w_down, moe_b_down)

    proj = _norm_mod_matmul(x, g_pre_mix[1], mods[1], gla_w_in[0], 0)
    gates = _gla_gates(x, g_pre_mix[1], mods[1], gla_w_g1_fwd[0], gla_w_g2_fwd[0], gla_b_g_fwd[0],
                       gla_w_g1_bwd[0], gla_w_g2_bwd[0], gla_b_g_bwd[0])
    of_c, ob_c, s_f, s_b = _gla_scan(proj, gates, seq_len=CTX_LEN, n_seq=N_CTX_SEQ, seq_off=0, out_state=True)
    of_l, ob_l = _gla_scan(proj, gates, seq_len=LAT_LEN, n_seq=N_LAT_SEQ, seq_off=T_CTX // LAT_LEN,
                           s_f0=state_fwd[:, 0], s_b0=state_bwd[:, 0], rotary=True)
    x = _gla_output(of_c, ob_c, of_l, ob_l, proj, gla_norm_g[0], x, mods[1], g_post_mix[1], gla_w_out[0])
    x = _moe(x, mods[1], g_pre_ffn[1], g_post_ffn[1], moe_w_router[1], moe_b_router[1],
             1, moe_w_up, moe_b_up, moe_w_down, moe_b_down)

    return (x[:T_CTX].reshape(N_CTX_SEQ, CTX_LEN, D), x[T_CTX:].reshape(N_LAT_SEQ, LAT_LEN, D),
            new_k, new_v, s_f[:, None], s_b[:, None])
```

```python
import functools

import numpy as np
import jax
import jax.numpy as jnp
from jax import lax
from jax.experimental import pallas as pl
from jax.experimental.pallas import tpu as pltpu

F32 = jnp.float32
BF16 = jnp.bfloat16
I32 = jnp.int32
HIGHEST = lax.Precision.HIGHEST

D = 1024
N_CTX_SEQ = 32
CTX_LEN = 256
N_LAT_SEQ = 2
LAT_LEN = 4096
T_CTX = N_CTX_SEQ * CTX_LEN
T_LAT = N_LAT_SEQ * LAT_LEN
T_ALL = T_CTX + T_LAT
GRID_W = 64
GRID_ROWS = LAT_LEN // GRID_W
NA_HEADS = 16
NA_HD = 64
NA_WIN_ROWS = 8
NA_WIN_COLS = 16
GLA_HEADS = 4
GLA_DK = 128
GLA_DV = 256
GLA_RANK = 16
GLA_GATE_NORM = 16.0
GLA_CHUNK = 64
ROPE_THETA = 10000.0
N_EXPERTS = 32
TOP_K = 4
D_FF = 1024
SWIGLU_LIMIT = 7.0
SWIGLU_ALPHA = 1.702
EPS = 1e-6
NEG_INF = -1e30

LANES = 128
TM = 512
GLA_BLOCK = 256
GLA_HG = 4
MOE_TM = 512
MOE_ROWS = T_ALL * TOP_K
MOE_TILES = MOE_ROWS // MOE_TM
MOE_VISITS = MOE_TILES + N_EXPERTS - 1
TD = 512
TC = 256
DMA_GROUP = 8
VMEM_LIMIT = 60 * 1024 * 1024

_NT = (((1,), (1,)), ((), ()))
_TN = (((0,), (0,)), ((), ()))


def _cparams(n_axes, vmem=None):
    return pltpu.CompilerParams(
        dimension_semantics=("arbitrary",) * n_axes,
        vmem_limit_bytes=VMEM_LIMIT if vmem is None else vmem)


def _mod_idx(i, tm):
    return jnp.maximum((i * tm) // LAT_LEN - 1, 0)


def _rms(x, g):
    return x * lax.rsqrt(jnp.mean(x * x, axis=-1, keepdims=True) + EPS) * g


def _norm_mod(x, g, m_ref, shift_row):
    sh = m_ref[0, shift_row:shift_row + 1, :]
    sc = m_ref[0, shift_row + 1:shift_row + 2, :]
    return _rms(x, g) * (1.0 + sc) + sh


def _mod_kernel(c_ref, w_ref, b_ref, o_ref):
    c = c_ref[...]
    s = c * jax.nn.sigmoid(c)
    o_ref[0] = jnp.dot(s, w_ref[0], precision=HIGHEST, preferred_element_type=F32) + b_ref[0]


def _modulation(cond, w_ada, b_ada):
    depth = w_ada.shape[0]
    out = pl.pallas_call(
        _mod_kernel,
        out_shape=jax.ShapeDtypeStruct((depth, 8, 6 * D), F32),
        grid=(depth, 6),
        in_specs=[pl.BlockSpec((8, D), lambda l, j: (0, 0)),
                  pl.BlockSpec((1, D, D), lambda l, j: (l, 0, j)),
                  pl.BlockSpec((1, 1, D), lambda l, j: (l, 0, j))],
        out_specs=pl.BlockSpec((1, 8, D), lambda l, j: (l, 0, j)),
        compiler_params=_cparams(2),
        name="adaln_modulation",
    )(cond, w_ada, b_ada.reshape(depth, 1, 6 * D))
    return out.reshape(depth, 8, 6, D)


def _nmm_kernel(x_ref, g_ref, m_ref, w_ref, o_ref, wb_ref, *, shift_row):
    @pl.when(pl.program_id(0) == 0)
    def _():
        wb_ref[...] = w_ref[...].astype(BF16)

    h = _norm_mod(x_ref[...], g_ref[...], m_ref, shift_row)
    o_ref[...] = jnp.dot(h.astype(BF16), wb_ref[...], preferred_element_type=F32)


def _norm_mod_matmul(x, g, mod, w, shift_row):
    t, n = x.shape[0], w.shape[1]
    return pl.pallas_call(
        functools.partial(_nmm_kernel, shift_row=shift_row),
        out_shape=jax.ShapeDtypeStruct((t, n), F32),
        grid=(t // TM,),
        in_specs=[pl.BlockSpec((TM, D), lambda i: (i, 0)),
                  pl.BlockSpec((1, D), lambda i: (0, 0)),
                  pl.BlockSpec((1, 6, D), lambda i: (_mod_idx(i, TM), 0, 0)),
                  pl.BlockSpec((D, n), lambda i: (0, 0), pipeline_mode=pl.Buffered(1))],
        out_specs=pl.BlockSpec((TM, n), lambda i: (i, 0)),
        scratch_shapes=[pltpu.VMEM((D, n), BF16)],
        compiler_params=_cparams(1),
        name="norm_mod_matmul",
    )(x, g.reshape(1, D), mod, w)


def _ctx_attn_kernel(qkv_ref, o_ref):
    lane = lax.broadcasted_iota(I32, (1, LANES), 1)
    scale = NA_HD ** -0.5
    for hp in range(NA_HEADS // 2):
        q = qkv_ref[0, :, hp * LANES:(hp + 1) * LANES] * scale
        k = qkv_ref[0, :, D + hp * LANES:D + (hp + 1) * LANES].astype(BF16)
        v = qkv_ref[0, :, 2 * D + hp * LANES:2 * D + (hp + 1) * LANES]
        acc = jnp.zeros((CTX_LEN, LANES), F32)
        for half in range(2):
            msk = (lane < NA_HD) if half == 0 else (lane >= NA_HD)
            qm = jnp.where(msk, q, 0.0).astype(BF16)
            s = lax.dot_general(qm, k, _NT, preferred_element_type=F32)
            p = jnp.exp(s - jnp.max(s, axis=-1, keepdims=True))
            l = jnp.sum(p, axis=-1, keepdims=True)
            vm = jnp.where(msk, v, 0.0).astype(BF16)
            acc = acc + jnp.dot(p.astype(BF16), vm, preferred_element_type=F32) / l
        o_ref[0, :, hp * LANES:(hp + 1) * LANES] = acc


def _ctx_attention(qkv):
    out = pl.pallas_call(
        _ctx_attn_kernel,
        out_shape=jax.ShapeDtypeStruct((N_CTX_SEQ, CTX_LEN, D), F32),
        grid=(N_CTX_SEQ,),
        in_specs=[pl.BlockSpec((1, CTX_LEN, 3 * D), lambda b: (b, 0, 0))],
        out_specs=pl.BlockSpec((1, CTX_LEN, D), lambda b: (b, 0, 0)),
        compiler_params=_cparams(1),
        name="context_attention",
    )(qkv.reshape(T_ALL // CTX_LEN, CTX_LEN, 3 * D))
    return out.reshape(T_CTX, D)


NA_QR = 8
NA_KR = 2 * NA_QR
NA_NBLK = GRID_ROWS // NA_QR
NA_N_OFF = 2 * NA_WIN_ROWS - 1
NA_SLAB_BOTH, NA_SLAB_LEFT, NA_SLAB_RIGHT, NA_SLAB_NONE = 0, NA_N_OFF - 1, 2 * NA_N_OFF - 1, 3 * NA_N_OFF - 1
NA_NSLAB = 3 * NA_N_OFF


def _na_slab_ids():
    ids = np.zeros((3, NA_QR, NA_KR // 2), np.int64)
    half = NA_WIN_ROWS // 2
    starts = ((0, 0), (NA_QR, NA_QR - half), (GRID_ROWS - NA_QR, GRID_ROWS - NA_KR))
    for v, (rb, kb) in enumerate(starts):
        for qr in range(NA_QR):
            qa = rb + qr
            r0 = min(max(qa - half, 0), GRID_ROWS - NA_WIN_ROWS)
            for m in range(NA_KR // 2):
                kl, kr = kb + 2 * m, kb + 2 * m + 1
                vl, vr = r0 <= kl < r0 + NA_WIN_ROWS, r0 <= kr < r0 + NA_WIN_ROWS
                ol, orr = kl - qa + NA_WIN_ROWS - 1, kr - qa + NA_WIN_ROWS - 1
                if vl and vr:
                    ids[v, qr, m] = NA_SLAB_BOTH + ol
                elif vl:
                    ids[v, qr, m] = NA_SLAB_LEFT + ol
                elif vr:
                    ids[v, qr, m] = NA_SLAB_RIGHT + orr
                else:
                    ids[v, qr, m] = NA_SLAB_NONE
    return ids


def _na_kernel(q_ref, k_ref, v_ref, ck_ref, cv_ref, slab_ref, o_ref, bias_scr):
    slab_ids = _na_slab_ids()
    for half in range(2):
        for v in range(3):
            for qr in range(NA_QR):
                for m in range(NA_KR // 2):
                    bias_scr[half, v, qr * GRID_W:(qr + 1) * GRID_W, m * LANES:(m + 1) * LANES] = (
                        slab_ref[half, int(slab_ids[v, qr, m])])

    lane = lax.broadcasted_iota(I32, (1, LANES), 1)
    masks = ((lane < NA_HD), (lane >= NA_HD))
    scale = NA_HD ** -0.5
    ck = ck_ref[0].astype(BF16)
    cv = cv_ref[0]
    cvm = [jnp.where(m, cv, 0.0).astype(BF16) for m in masks]
    nq = NA_QR * GRID_W
    nk = NA_KR * GRID_W

    def body(blk, carry):
        rb = blk * NA_QR
        kb = jnp.clip(rb - NA_WIN_ROWS // 2, 0, GRID_ROWS - NA_KR)
        layout = jnp.where(blk == 0, 0, jnp.where(blk == NA_NBLK - 1, 2, 1))
        qs = pl.ds(pl.multiple_of(rb * GRID_W, nq), nq)
        ws = pl.ds(pl.multiple_of(kb * GRID_W, GRID_W * NA_WIN_ROWS // 2), nk)
        q = q_ref[0, qs, :] * scale
        kw = k_ref[0, ws, :].astype(BF16)
        vw = v_ref[0, ws, :]
        halves = range(2)
        qms = [jnp.where(masks[h], q, 0.0).astype(BF16) for h in halves]
        sws = [lax.dot_general(qms[h], kw, _NT, preferred_element_type=F32) + bias_scr[h, pl.ds(layout, 1)][0]
               for h in halves]
        scs = [lax.dot_general(qms[h], ck, _NT, preferred_element_type=F32) for h in halves]
        ms = [jnp.maximum(jnp.max(sws[h], axis=-1, keepdims=True), jnp.max(scs[h], axis=-1, keepdims=True))
              for h in halves]
        pws = [jnp.exp(sws[h] - ms[h]) for h in halves]
        pcs = [jnp.exp(scs[h] - ms[h]) for h in halves]
        ls = [jnp.sum(pws[h], axis=-1, keepdims=True) + jnp.sum(pcs[h], axis=-1, keepdims=True) for h in halves]
        vms = [jnp.where(masks[h], vw, 0.0).astype(BF16) for h in halves]
        os_ = [jnp.dot(pws[h].astype(BF16), vms[h], preferred_element_type=F32)
               + jnp.dot(pcs[h].astype(BF16), cvm[h], preferred_element_type=F32) for h in halves]
        o_ref[0, qs, :] = os_[0] / ls[0] + os_[1] / ls[1]
        return carry

    lax.fori_loop(0, NA_NBLK, body, 0)


def _na_slab_table(rpb):
    cidx = np.arange(GRID_W)
    col_start = np.clip(cidx - NA_WIN_COLS // 2, 0, GRID_W - NA_WIN_COLS)
    col_ok = (cidx[None, :] >= col_start[:, None]) & (cidx[None, :] < col_start[:, None] + NA_WIN_COLS)
    coff = np.clip(cidx[None, :] - cidx[:, None], -(NA_WIN_COLS - 1), NA_WIN_COLS - 1) + NA_WIN_COLS - 1
    n_coff = 2 * NA_WIN_COLS - 1
    onehot = (coff[None, :, :] == np.arange(n_coff)[:, None, None]).astype(np.float32)
    tab = jnp.einsum("hrj,jcw->hrcw", rpb.astype(F32), onehot, precision=HIGHEST)
    tab = jnp.where(col_ok[None, None], tab, NEG_INF)
    neg = jnp.full_like(tab, NEG_INF)
    both = jnp.concatenate([tab[:, :-1], tab[:, 1:]], axis=-1)
    left = jnp.concatenate([tab, neg], axis=-1)
    right = jnp.concatenate([neg, tab], axis=-1)
    none = jnp.concatenate([neg[:, :1], neg[:, :1]], axis=-1)
    return jnp.concatenate([both, left, right, none], axis=1)


def _na_attention(qkv, cache_k, cache_v, rpb):
    off = T_CTX // LAT_LEN
    qkv3 = qkv.reshape(T_ALL // LAT_LEN, LAT_LEN, 3 * D)
    ck = cache_k.reshape(N_LAT_SEQ, CTX_LEN, D)
    cv = cache_v.reshape(N_LAT_SEQ, CTX_LEN, D)
    slabs = _na_slab_table(rpb)
    nh = D // LANES
    out = pl.pallas_call(
        _na_kernel,
        out_shape=jax.ShapeDtypeStruct((N_LAT_SEQ, LAT_LEN, D), F32),
        grid=(N_LAT_SEQ, nh),
        in_specs=[pl.BlockSpec((1, LAT_LEN, LANES), lambda b, h: (b + off, 0, h)),
                  pl.BlockSpec((1, LAT_LEN, LANES), lambda b, h: (b + off, 0, nh + h)),
                  pl.BlockSpec((1, LAT_LEN, LANES), lambda b, h: (b + off, 0, 2 * nh + h)),
                  pl.BlockSpec((1, CTX_LEN, LANES), lambda b, h: (b, 0, h)),
                  pl.BlockSpec((1, CTX_LEN, LANES), lambda b, h: (b, 0, h)),
                  pl.BlockSpec((2, NA_NSLAB, GRID_W, 2 * GRID_W), lambda b, h: (h, 0, 0, 0))],
        out_specs=pl.BlockSpec((1, LAT_LEN, LANES), lambda b, h: (b, 0, h)),
        scratch_shapes=[pltpu.VMEM((2, 3, NA_QR * GRID_W, NA_KR * GRID_W), F32)],
        compiler_params=_cparams(2),
        name="neighbourhood_attention",
    )(qkv3, qkv3, qkv3, ck, cv, slabs)
    return out.reshape(T_LAT, D)


CTX_TILES = T_CTX // TM


def _ctx_part(i):
    return (jnp.minimum(i, CTX_TILES - 1), 0)


def _lat_part(i):
    return (jnp.maximum(i - CTX_TILES, 0), 0)


def _proj_res_kernel(ac_ref, al_ref, x_ref, m_ref, g_ref, w_ref, o_ref, wb_ref, *, gate_row):
    i = pl.program_id(0)

    @pl.when(i == 0)
    def _():
        wb_ref[...] = w_ref[...].astype(BF16)

    a = jnp.where(i < CTX_TILES, ac_ref[...], al_ref[...])
    y = jnp.dot(a.astype(BF16), wb_ref[...], preferred_element_type=F32)
    o_ref[...] = x_ref[...] + m_ref[0, gate_row:gate_row + 1, :] * _rms(y, g_ref[...])


def _proj_residual(a_ctx, a_lat, x, mod, g_post, w, gate_row):
    return pl.pallas_call(
        functools.partial(_proj_res_kernel, gate_row=gate_row),
        out_shape=jax.ShapeDtypeStruct((T_ALL, D), F32),
        grid=(T_ALL // TM,),
        in_specs=[pl.BlockSpec((TM, D), _ctx_part),
                  pl.BlockSpec((TM, D), _lat_part),
                  pl.BlockSpec((TM, D), lambda i: (i, 0)),
                  pl.BlockSpec((1, 6, D), lambda i: (_mod_idx(i, TM), 0, 0)),
                  pl.BlockSpec((1, D), lambda i: (0, 0)),
                  pl.BlockSpec((D, D), lambda i: (0, 0))],
        out_specs=pl.BlockSpec((TM, D), lambda i: (i, 0)),
        scratch_shapes=[pltpu.VMEM((D, D), BF16)],
        compiler_params=_cparams(1),
        name="proj_residual",
    )(a_ctx, a_lat, x, mod, g_post.reshape(1, D), w)


def _gla_gate_kernel(x_ref, g_ref, m_ref, w1_ref, w2_ref, b_ref, o_ref):
    h = _norm_mod(x_ref[...], g_ref[...], m_ref, 0)
    z = jnp.dot(h.astype(BF16), w1_ref[...].astype(BF16), preferred_element_type=F32)
    y = jnp.dot(z, w2_ref[...], precision=HIGHEST, preferred_element_type=F32) + b_ref[...]
    o_ref[...] = (jnp.minimum(y, 0.0) - jnp.log(1.0 + jnp.exp(-jnp.abs(y)))) * (1.0 / GLA_GATE_NORM)


def _gla_gates(x, g, mod, w1f, w2f, bf, w1b, w2b, bb):
    hk = GLA_HEADS * GLA_DK
    w1 = jnp.zeros((D, LANES), F32).at[:, :GLA_RANK].set(w1f).at[:, GLA_RANK:2 * GLA_RANK].set(w1b)
    w2 = jnp.zeros((LANES, 2 * hk), F32).at[:GLA_RANK, :hk].set(w2f).at[GLA_RANK:2 * GLA_RANK, hk:].set(w2b)
    b = jnp.concatenate([bf, bb]).reshape(1, 2 * hk)
    return pl.pallas_call(
        _gla_gate_kernel,
        out_shape=jax.ShapeDtypeStruct((T_ALL, 2 * hk), F32),
        grid=(T_ALL // TM,),
        in_specs=[pl.BlockSpec((TM, D), lambda i: (i, 0)),
                  pl.BlockSpec((1, D), lambda i: (0, 0)),
                  pl.BlockSpec((1, 6, D), lambda i: (_mod_idx(i, TM), 0, 0)),
                  pl.BlockSpec((D, LANES), lambda i: (0, 0)),
                  pl.BlockSpec((LANES, 2 * hk), lambda i: (0, 0)),
                  pl.BlockSpec((1, 2 * hk), lambda i: (0, 0))],
        out_specs=pl.BlockSpec((TM, 2 * hk), lambda i: (i, 0)),
        compiler_params=_cparams(1),
        name="gla_gates",
    )(x, g.reshape(1, D), mod, w1, w2, b)


def _rope(x, cos, sin_signed):
    lane = lax.broadcasted_iota(I32, (1, LANES), 1)
    partner = jnp.where((lane % 64) < 32, pltpu.roll(x, LANES - 32, 1), pltpu.roll(x, 32, 1))
    return x * cos + partner * sin_signed


def _gla_chunks(chunks, states):
    L = GLA_CHUNK
    cums = [jnp.dot(c["tri"].astype(F32), c["g"], precision=HIGHEST, preferred_element_type=F32) for c in chunks]
    cls = [cum[L - 1:L, :] if c["forward"] else cum[0:1, :] for c, cum in zip(chunks, cums)]
    q_decs = [(c["q"] * (GLA_DK ** -0.5) * jnp.exp(cum)).astype(BF16) for c, cum in zip(chunks, cums)]
    k_decs = [(c["k"] * jnp.exp(-cum)).astype(BF16) for c, cum in zip(chunks, cums)]
    k_rems = [(c["k"] * jnp.exp(cl - cum)).astype(BF16) for c, cum, cl in zip(chunks, cums, cls)]
    vbs = [c["v"].astype(BF16) for c in chunks]
    kv_ts = [lax.dot_general(vb, k_rem, _TN, preferred_element_type=F32) for vb, k_rem in zip(vbs, k_rems)]
    a_s = [lax.dot_general(q_dec, k_dec, _NT, preferred_element_type=F32) for q_dec, k_dec in zip(q_decs, k_decs)]
    a_s = [jnp.where(c["tri"], a, 0.0).astype(BF16) for c, a in zip(chunks, a_s)]
    outs = [jnp.dot(a, vb, preferred_element_type=F32) for a, vb in zip(a_s, vbs)]
    states = dict(states)
    for i, c in enumerate(chunks):
        st = states[c["scan"]]
        outs[i] = outs[i] + lax.dot_general(q_decs[i], st.astype(BF16), _NT, preferred_element_type=F32)
        states[c["scan"]] = jnp.exp(cls[i]) * st + kv_ts[i]
    return outs, states


def _gla_kernel(*refs, rotary, has_init, out_state, nblk):
    refs = list(refs)
    qf, kf, vf, gf, qb, kb, vb, gb = refs[:8]
    del refs[:8]
    if rotary:
        cosf, sinf, cosb, sinb = refs[:4]
        del refs[:4]
    if has_init:
        sf0, sb0 = refs[:2]
        del refs[:2]
    of, ob = refs[:2]
    del refs[:2]
    if out_state:
        sfo, sbo = refs[:2]
        del refs[:2]
    st_f, st_b = refs

    j = pl.program_id(2)

    @pl.when(j == 0)
    def _():
        if has_init:
            for hh in range(GLA_HG):
                st_f[hh] = sf0[0, hh].T
                st_b[hh] = sb0[0, hh].T
        else:
            st_f[...] = jnp.zeros_like(st_f)
            st_b[...] = jnp.zeros_like(st_b)

    L = GLA_CHUNK
    row = lax.broadcasted_iota(I32, (L, L), 0)
    col = lax.broadcasted_iota(I32, (L, L), 1)
    tri_f = col <= row
    tri_b = col >= row
    nchunk = GLA_BLOCK // L

    chunks, where, states = [], [], {}
    for hh in range(GLA_HG):
        ks = slice(hh * GLA_DK, (hh + 1) * GLA_DK)
        vs = slice(hh * GLA_DV, (hh + 1) * GLA_DV)
        states[(hh, True)] = st_f[hh]
        states[(hh, False)] = st_b[hh]
        for forward in (True, False):
            q_ref, k_ref, v_ref, g_ref = (qf, kf, vf, gf) if forward else (qb, kb, vb, gb)
            for c in (range(nchunk) if forward else reversed(range(nchunk))):
                sl = slice(c * L, (c + 1) * L)
                q, k = q_ref[0, sl, ks], k_ref[0, sl, ks]
                if rotary:
                    cos, sin = (cosf, sinf) if forward else (cosb, sinb)
                    q = _rope(q, cos[sl, :], sin[sl, :])
                    k = _rope(k, cos[sl, :], sin[sl, :])
                chunks.append(dict(q=q, k=k, v=v_ref[0, sl, vs], g=g_ref[0, sl, ks], forward=forward,
                                   tri=tri_f if forward else tri_b, scan=(hh, forward)))
                where.append((of if forward else ob, sl, vs))
    outs, states = _gla_chunks(chunks, states)
    for (o_ref, sl, vs), o in zip(where, outs):
        o_ref[0, sl, vs] = o
    for hh in range(GLA_HG):
        st_f[hh] = states[(hh, True)]
        st_b[hh] = states[(hh, False)]

    if out_state:
        @pl.when(j == nblk - 1)
        def _():
            for hh in range(GLA_HG):
                sfo[0, hh] = st_f[hh].T
                sbo[0, hh] = st_b[hh].T


def _rope_tables(n):
    t = np.arange(n)
    n_freq = GLA_DK // 4
    inv = ROPE_THETA ** (-np.arange(n_freq, dtype=np.float64) / n_freq)
    ang_r = (t // GRID_W).astype(np.float64)[:, None] * inv[None, :]
    ang_c = (t % GRID_W).astype(np.float64)[:, None] * inv[None, :]
    cos = np.concatenate([np.cos(ang_r), np.cos(ang_r), np.cos(ang_c), np.cos(ang_c)], axis=1)
    sin = np.concatenate([-np.sin(ang_r), np.sin(ang_r), -np.sin(ang_c), np.sin(ang_c)], axis=1)
    return jnp.asarray(cos, F32), jnp.asarray(sin, F32)


def _gla_scan(proj, gates, *, seq_len, n_seq, seq_off, s_f0=None, s_b0=None, rotary=False, out_state=False):
    nseq_all = T_ALL // seq_len
    nblk = seq_len // GLA_BLOCK
    nh = GLA_HEADS
    proj3 = proj.reshape(nseq_all, seq_len, 3 * D)
    g3 = gates.reshape(nseq_all, seq_len, 2 * nh * GLA_DK)
    has_init = s_f0 is not None

    def fwd(c0):
        return lambda b, h, j: (b + seq_off, j, c0 + h)

    def bwd(c0):
        return lambda b, h, j: (b + seq_off, nblk - 1 - j, c0 + h)

    ngrp = nh // GLA_HG
    qk = (1, GLA_BLOCK, GLA_HG * GLA_DK)
    vv = (1, GLA_BLOCK, GLA_HG * GLA_DV)
    in_specs = [pl.BlockSpec(qk, fwd(0)), pl.BlockSpec(qk, fwd(ngrp)), pl.BlockSpec(vv, fwd(ngrp)), pl.BlockSpec(qk, fwd(0)),
                pl.BlockSpec(qk, bwd(0)), pl.BlockSpec(qk, bwd(ngrp)), pl.BlockSpec(vv, bwd(ngrp)), pl.BlockSpec(qk, bwd(ngrp))]
    args = [proj3, proj3, proj3, g3, proj3, proj3, proj3, g3]
    if rotary:
        cos, sin = _rope_tables(seq_len)
        tab = (GLA_BLOCK, GLA_DK)
        in_specs += [pl.BlockSpec(tab, lambda b, h, j: (j, 0)), pl.BlockSpec(tab, lambda b, h, j: (j, 0)),
                     pl.BlockSpec(tab, lambda b, h, j: (nblk - 1 - j, 0)), pl.BlockSpec(tab, lambda b, h, j: (nblk - 1 - j, 0))]
        args += [cos, sin, cos, sin]
    st = (1, GLA_HG, GLA_DK, GLA_DV)
    if has_init:
        in_specs += [pl.BlockSpec(st, lambda b, h, j: (b, h, 0, 0))] * 2
        args += [s_f0, s_b0]
    out_shape = [jax.ShapeDtypeStruct((n_seq, seq_len, D), F32)] * 2
    out_specs = [pl.BlockSpec(vv, lambda b, h, j: (b, j, h)),
                 pl.BlockSpec(vv, lambda b, h, j: (b, nblk - 1 - j, h))]
    if out_state:
        out_shape += [jax.ShapeDtypeStruct((n_seq, nh, GLA_DK, GLA_DV), F32)] * 2
        out_specs += [pl.BlockSpec(st, lambda b, h, j: (b, h, 0, 0))] * 2
    res = pl.pallas_call(
        functools.partial(_gla_kernel, rotary=rotary, has_init=has_init, out_state=out_state, nblk=nblk),
        out_shape=out_shape,
        grid=(n_seq, ngrp, nblk),
        in_specs=in_specs,
        out_specs=out_specs,
        scratch_shapes=[pltpu.VMEM((GLA_HG, GLA_DV, GLA_DK), F32)] * 2,
        compiler_params=_cparams(3),
        name="gla_scan_rope" if rotary else "gla_scan",
    )(*args)
    of, ob = res[0].reshape(n_seq * seq_len, D), res[1].reshape(n_seq * seq_len, D)
    if out_state:
        return of, ob, res[2], res[3]
    return of, ob


def _gla_out_kernel(ofc_ref, obc_ref, ofl_ref, obl_ref, r_ref, ng_ref, x_ref, m_ref, g_ref, w_ref, o_ref, wb_ref):
    i = pl.program_id(0)

    @pl.when(i == 0)
    def _():
        wb_ref[...] = w_ref[...].astype(BF16)

    o = jnp.where(i < CTX_TILES, ofc_ref[...] + obc_ref[...], ofl_ref[...] + obl_ref[...])
    r = r_ref[...]
    ng = ng_ref[...]
    parts = []
    for h in range(GLA_HEADS):
        oh = o[:, h * GLA_DV:(h + 1) * GLA_DV]
        parts.append(_rms(oh, ng))
    a = jnp.concatenate(parts, axis=1) * (r * jax.nn.sigmoid(r))
    y = jnp.dot(a.astype(BF16), wb_ref[...], preferred_element_type=F32)
    o_ref[...] = x_ref[...] + m_ref[0, 2:3, :] * _rms(y, g_ref[...])


def _gla_output(of_ctx, ob_ctx, of_lat, ob_lat, proj, norm_g, x, mod, g_post, w):
    return pl.pallas_call(
        _gla_out_kernel,
        out_shape=jax.ShapeDtypeStruct((T_ALL, D), F32),
        grid=(T_ALL // TM,),
        in_specs=[pl.BlockSpec((TM, D), _ctx_part),
                  pl.BlockSpec((TM, D), _ctx_part),
                  pl.BlockSpec((TM, D), _lat_part),
                  pl.BlockSpec((TM, D), _lat_part),
                  pl.BlockSpec((TM, D), lambda i: (i, 2)),
                  pl.BlockSpec((1, GLA_DV), lambda i: (0, 0)),
                  pl.BlockSpec((TM, D), lambda i: (i, 0)),
                  pl.BlockSpec((1, 6, D), lambda i: (_mod_idx(i, TM), 0, 0)),
                  pl.BlockSpec((1, D), lambda i: (0, 0)),
                  pl.BlockSpec((D, D), lambda i: (0, 0))],
        out_specs=pl.BlockSpec((TM, D), lambda i: (i, 0)),
        scratch_shapes=[pltpu.VMEM((D, D), BF16)],
        compiler_params=_cparams(1),
        name="gla_output",
    )(of_ctx, ob_ctx, of_lat, ob_lat, proj, norm_g.reshape(1, GLA_DV), x, mod, g_post.reshape(1, D), w)


def _router_kernel(x_ref, g_ref, m_ref, wr_ref, br_ref, h_ref, ri_ref, rw_ref, cnt_ref, cnt_scr):
    @pl.when(pl.program_id(0) == 0)
    def _():
        cnt_scr[...] = jnp.zeros_like(cnt_scr)

    h = _norm_mod(x_ref[...], g_ref[...], m_ref, 3)
    h_ref[...] = h
    logits = jnp.dot(h, wr_ref[...], precision=HIGHEST, preferred_element_type=F32) + br_ref[...]
    lane = lax.broadcasted_iota(I32, (TM, LANES), 1)
    lane_f = lane.astype(F32)
    cur = jnp.where(lane < N_EXPERTS, logits, -jnp.inf)
    vals, sels = [], []
    hot = jnp.zeros((TM, LANES), F32)
    for _ in range(TOP_K):
        m = jnp.max(cur, axis=-1, keepdims=True)
        idx = jnp.min(jnp.where(cur == m, lane_f, float(LANES)), axis=-1, keepdims=True)
        sel = lane_f == idx
        vals.append(m)
        sels.append((idx, sel))
        hot = hot + sel.astype(F32)
        cur = jnp.where(sel, -jnp.inf, cur)
    ex = [jnp.exp(v - vals[0]) for v in vals]
    den = ex[0] + ex[1] + ex[2] + ex[3]
    r_i = lax.broadcasted_iota(I32, (TM, TM), 0)
    c_i = lax.broadcasted_iota(I32, (TM, TM), 1)
    before = (c_i < r_i).astype(BF16)
    prefix = jnp.dot(before, hot.astype(BF16), preferred_element_type=F32) + cnt_scr[0:1, :]
    ri = jnp.zeros((TM, LANES), F32)
    rw = jnp.zeros((TM, LANES), F32)
    for k in range(TOP_K):
        idx, sel = sels[k]
        rank = jnp.sum(jnp.where(sel, prefix, 0.0), axis=-1, keepdims=True)
        ri = jnp.where(lane == k, idx, ri)
        ri = jnp.where(lane == TOP_K + k, rank, ri)
        rw = jnp.where(lane == k, ex[k] / den, rw)
    ri_ref[...] = ri.astype(I32)
    rw_ref[...] = rw
    cnt = cnt_scr[...] + jnp.sum(hot, axis=0, keepdims=True)
    cnt_scr[...] = cnt
    cnt_ref[...] = cnt


def _router(x, g, mod, w_router, b_router):
    wr = jnp.zeros((D, LANES), F32).at[:, :N_EXPERTS].set(w_router)
    br = jnp.zeros((1, LANES), F32).at[0, :N_EXPERTS].set(b_router)
    return pl.pallas_call(
        _router_kernel,
        out_shape=[jax.ShapeDtypeStruct((T_ALL, D), F32),
                   jax.ShapeDtypeStruct((T_ALL, LANES), I32),
                   jax.ShapeDtypeStruct((T_ALL, LANES), F32),
                   jax.ShapeDtypeStruct((8, LANES), F32)],
        grid=(T_ALL // TM,),
        in_specs=[pl.BlockSpec((TM, D), lambda i: (i, 0)),
                  pl.BlockSpec((1, D), lambda i: (0, 0)),
                  pl.BlockSpec((1, 6, D), lambda i: (_mod_idx(i, TM), 0, 0)),
                  pl.BlockSpec((D, LANES), lambda i: (0, 0)),
                  pl.BlockSpec((1, LANES), lambda i: (0, 0))],
        out_specs=[pl.BlockSpec((TM, D), lambda i: (i, 0)),
                   pl.BlockSpec((TM, LANES), lambda i: (i, 0)),
                   pl.BlockSpec((TM, LANES), lambda i: (i, 0)),
                   pl.BlockSpec((8, LANES), lambda i: (0, 0))],
        scratch_shapes=[pltpu.VMEM((8, LANES), F32)],
        compiler_params=_cparams(1),
        name="moe_router",
    )(x, g.reshape(1, D), mod, wr, br)


def _dispatch_kernel(dest_ref, h_ref, xs_ref, sem):
    base = pl.program_id(0) * (TD * TOP_K)

    def row_copy(j, u, k):
        i0 = pl.multiple_of(j * DMA_GROUP, DMA_GROUP)
        d = dest_ref[base + (i0 + u) * TOP_K + k]
        return pltpu.make_async_copy(h_ref.at[pl.ds(i0, DMA_GROUP)].at[pl.ds(u, 1)], xs_ref.at[pl.ds(d, 1)], sem)

    def start(j, c):
        for u in range(DMA_GROUP):
            for k in range(TOP_K):
                row_copy(j, u, k).start(priority=k % 2)
        return c

    def wait(j, c):
        for u in range(DMA_GROUP):
            for k in range(TOP_K):
                row_copy(j, u, k).wait()
        return c

    lax.fori_loop(0, TD // DMA_GROUP, start, 0)
    lax.fori_loop(0, TD // DMA_GROUP, wait, 0)


def _dispatch(dest, h):
    return pl.pallas_call(
        _dispatch_kernel,
        out_shape=jax.ShapeDtypeStruct((MOE_ROWS, D), F32),
        grid_spec=pltpu.PrefetchScalarGridSpec(
            num_scalar_prefetch=1,
            grid=(T_ALL // TD,),
            in_specs=[pl.BlockSpec((TD, D), lambda i, dest: (i, 0))],
            out_specs=pl.BlockSpec(memory_space=pl.ANY),
            scratch_shapes=[pltpu.SemaphoreType.DMA(())]),
        compiler_params=_cparams(1),
        name="moe_dispatch",
    )(dest, h)


def _gmm_kernel(vt_ref, vg_ref, lo_ref, hi_ref, first_ref, x_ref, wu_ref, bu_ref, wd_ref, bd_ref, y_ref, wub, wdb):
    v = pl.program_id(0)
    g = vg_ref[v]
    new_group = jnp.logical_or(v == 0, vg_ref[jnp.maximum(v - 1, 0)] != g)

    @pl.when(new_group)
    def _():
        wub[...] = wu_ref[0, 0].astype(BF16)
        wdb[...] = wd_ref[0, 0].astype(BF16)

    lo = lo_ref[v]
    hi = hi_ref[v]

    def expert_rows():
        u = jnp.dot(x_ref[...].astype(BF16), wub[...], preferred_element_type=F32) + bu_ref[0, 0]
        gate = jnp.minimum(u[:, :D_FF], SWIGLU_LIMIT)
        lin = jnp.clip(u[:, D_FF:], -SWIGLU_LIMIT, SWIGLU_LIMIT)
        act = gate * jax.nn.sigmoid(SWIGLU_ALPHA * gate) * (lin + 1.0)
        y = jnp.dot(act.astype(BF16), wdb[...], preferred_element_type=F32) + bd_ref[0, 0]
        row = lax.broadcasted_iota(I32, (MOE_TM, 1), 0)
        return y, jnp.logical_and(row >= lo, row < hi)

    @pl.when(jnp.logical_and(hi > lo, first_ref[v] == 1))
    def _():
        y, mine = expert_rows()
        y_ref[...] = jnp.where(mine, y, 0.0)

    @pl.when(jnp.logical_and(hi > lo, first_ref[v] == 0))
    def _():
        y, mine = expert_rows()
        y_ref[...] = jnp.where(mine, y, y_ref[...])


def _moe_visits(counts):
    ends = jnp.cumsum(counts)
    starts = ends - counts
    first_tile = starts // MOE_TM
    last_tile = (ends - 1) // MOE_TM
    ntl = jnp.where(counts > 0, last_tile - first_tile + 1, 0)
    vend = jnp.cumsum(ntl)
    vstart = vend - ntl
    total = vend[-1]
    v = jnp.arange(MOE_VISITS, dtype=I32)
    vc = jnp.minimum(v, total - 1)
    grp = jnp.minimum(jnp.sum((vend[None, :] <= vc[:, None]).astype(I32), axis=1), N_EXPERTS - 1)
    tile = first_tile[grp] + (vc - vstart[grp])
    valid = v < total
    lo = jnp.where(valid, jnp.clip(starts[grp] - tile * MOE_TM, 0, MOE_TM), 0)
    hi = jnp.where(valid, jnp.clip(ends[grp] - tile * MOE_TM, 0, MOE_TM), 0)
    prev_tile = jnp.concatenate([jnp.full((1,), -1, I32), tile[:-1]])
    first = jnp.logical_and(valid, tile != prev_tile).astype(I32)
    return tile.astype(I32), grp, lo.astype(I32), hi.astype(I32), first, starts


def _grouped_mlp(visits, xs, layer, w_up, b_up, w_down, b_down):
    tile, grp, lo, hi, first = visits
    depth = w_up.shape[0]
    return pl.pallas_call(
        _gmm_kernel,
        out_shape=jax.ShapeDtypeStruct((MOE_ROWS, D), F32),
        grid_spec=pltpu.PrefetchScalarGridSpec(
            num_scalar_prefetch=5,
            grid=(MOE_VISITS,),
            in_specs=[pl.BlockSpec((MOE_TM, D), lambda v, vt, vg, lo, hi, fi: (vt[v], 0)),
                      pl.BlockSpec((1, 1, D, 2 * D_FF), lambda v, vt, vg, lo, hi, fi: (layer, vg[v], 0, 0)),
                      pl.BlockSpec((1, 1, 1, 2 * D_FF), lambda v, vt, vg, lo, hi, fi: (layer, vg[v], 0, 0)),
                      pl.BlockSpec((1, 1, D_FF, D), lambda v, vt, vg, lo, hi, fi: (layer, vg[v], 0, 0)),
                      pl.BlockSpec((1, 1, 1, D), lambda v, vt, vg, lo, hi, fi: (layer, vg[v], 0, 0))],
            out_specs=pl.BlockSpec((MOE_TM, D), lambda v, vt, vg, lo, hi, fi: (vt[v], 0)),
            scratch_shapes=[pltpu.VMEM((D, 2 * D_FF), BF16), pltpu.VMEM((D_FF, D), BF16)]),
        compiler_params=_cparams(1),
        name="moe_grouped_mlp",
    )(tile, grp, lo, hi, first, xs, w_up, b_up.reshape(depth, N_EXPERTS, 1, 2 * D_FF), w_down,
      b_down.reshape(depth, N_EXPERTS, 1, D))


def _combine_kernel(dest_ref, ys_ref, rw_ref, x_ref, m_ref, g_ref, o_ref, buf, sem):
    step = pl.program_id(0)
    slot = step % 2

    def row_copy(s, j, u, k):
        i0 = pl.multiple_of(j * DMA_GROUP, DMA_GROUP)
        d = dest_ref[s * (TC * TOP_K) + (i0 + u) * TOP_K + k]
        return pltpu.make_async_copy(ys_ref.at[pl.ds(d, 1)], buf.at[s % 2, k, pl.ds(i0, DMA_GROUP)].at[pl.ds(u, 1)],
                                     sem.at[s % 2])

    def issue(s):
        def start(j, c):
            for u in range(DMA_GROUP):
                for k in range(TOP_K):
                    row_copy(s, j, u, k).start(priority=k % 2)
            return c

        lax.fori_loop(0, TC // DMA_GROUP, start, 0)

    @pl.when(step == 0)
    def _():
        issue(step)

    @pl.when(step + 1 < pl.num_programs(0))
    def _():
        issue(step + 1)

    def wait(j, c):
        for u in range(DMA_GROUP):
            for k in range(TOP_K):
                row_copy(step, j, u, k).wait()
        return c

    lax.fori_loop(0, TC // DMA_GROUP, wait, 0)
    w = rw_ref[...]
    f = w[:, 0:1] * buf[slot, 0]
    for k in range(1, TOP_K):
        f = f + w[:, k:k + 1] * buf[slot, k]
    o_ref[...] = x_ref[...] + m_ref[0, 5:6, :] * _rms(f, g_ref[...])


def _combine(dest, ys, rw, x, mod, g_post):
    return pl.pallas_call(
        _combine_kernel,
        out_shape=jax.ShapeDtypeStruct((T_ALL, D), F32),
        grid_spec=pltpu.PrefetchScalarGridSpec(
            num_scalar_prefetch=1,
            grid=(T_ALL // TC,),
            in_specs=[pl.BlockSpec(memory_space=pl.ANY),
                      pl.BlockSpec((TC, LANES), lambda i, dest: (i, 0)),
                      pl.BlockSpec((TC, D), lambda i, dest: (i, 0)),
                      pl.BlockSpec((1, 6, D), lambda i, dest: (_mod_idx(i, TC), 0, 0)),
                      pl.BlockSpec((1, D), lambda i, dest: (0, 0))],
            out_specs=pl.BlockSpec((TC, D), lambda i, dest: (i, 0)),
            scratch_shapes=[pltpu.VMEM((2, TOP_K, TC, D), F32), pltpu.SemaphoreType.DMA((2,))]),
        compiler_params=_cparams(1),
        name="moe_combine",
    )(dest, ys, rw, x, mod, g_post.reshape(1, D))


def _moe(x, mod, g_pre, g_post, w_router, b_router, layer, w_up, b_up, w_down, b_down):
    h, ri, rw, cnt = _router(x, g_pre, mod, w_router, b_router)
    counts = cnt[0, :N_EXPERTS].astype(I32)
    tile, grp, lo, hi, first, starts = _moe_visits(counts)
    dest = (starts[ri[:, :TOP_K]] + ri[:, TOP_K:2 * TOP_K]).reshape(-1).astype(I32)
    xs = _dispatch(dest, h)
    ys = _grouped_mlp((tile, grp, lo, hi, first), xs, layer, w_up, b_up, w_down, b_down)
    return _combine(dest, ys, rw, x, mod, g_post)


def kernel(x_prompt, x_sample, cache_k, cache_v, state_fwd, state_bwd, c, c_ctx, w_ada, b_ada, g_pre_mix, g_post_mix, g_pre_ffn, g_post_ffn, na_w_qkv, na_rpb, na_w_out, gla_w_in, gla_w_g1_fwd, gla_w_g2_fwd, gla_b_g_fwd, gla_w_g1_bwd, gla_w_g2_bwd, gla_b_g_bwd, gla_norm_g, gla_w_out, moe_w_router, moe_b_router, moe_w_up, moe_b_up, moe_w_down, moe_b_down):
    x = jnp.concatenate([x_prompt.reshape(T_CTX, D), x_sample.reshape(T_LAT, D)], axis=0)
    cond = jnp.concatenate([c_ctx[None, :], c, jnp.zeros((8 - 1 - N_LAT_SEQ, D), F32)], axis=0)
    mods = _modulation(cond, w_ada, b_ada)

    qkv = _norm_mod_matmul(x, g_pre_mix[0], mods[0], na_w_qkv[0], 0)
    a_ctx = _ctx_attention(qkv)
    a_lat = _na_attention(qkv, cache_k[:, 0], cache_v[:, 0], na_rpb[0])
    x = _proj_residual(a_ctx, a_lat, x, mods[0], g_post_mix[0], na_w_out[0], 2)
    new_k = qkv[:T_CTX, D:2 * D].reshape(N_CTX_SEQ, 1, CTX_LEN, NA_HEADS, NA_HD)
    new_v = qkv[:T_CTX, 2 * D:].reshape(N_CTX_SEQ, 1, CTX_LEN, NA_HEADS, NA_HD)
    x = _moe(x, mods[0], g_pre_ffn[0], g_post_ffn[0], moe_w_router[0], moe_b_router[0],
             0, moe_w_up, moe_b_up, moe_w_down, moe_b_down)

    proj = _norm_mod_matmul(x, g_pre_mix[1], mods[1], gla_w_in[0], 0)
    gates = _gla_gates(x, g_pre_mix[1], mods[1], gla_w_g1_fwd[0], gla_w_g2_fwd[0], gla_b_g_fwd[0],
                       gla_w_g1_bwd[0], gla_w_g2_bwd[0], gla_b_g_bwd[0])
    of_c, ob_c, s_f, s_b = _gla_scan(proj, gates, seq_len=CTX_LEN, n_seq=N_CTX_SEQ, seq_off=0, out_state=True)
    of_l, ob_l = _gla_scan(proj, gates, seq_len=LAT_LEN, n_seq=N_LAT_SEQ, seq_off=T_CTX // LAT_LEN,
                           s_f0=state_fwd[:, 0], s_b0=state_bwd[:, 0], rotary=True)
    x = _gla_output(of_c, ob_c, of_l, ob_l, proj, gla_norm_g[0], x, mods[1], g_post_mix[1], gla_w_out[0])
    x = _moe(x, mods[1], g_pre_ffn[1], g_post_ffn[1], moe_w_router[1], moe_b_router[1],
             1, moe_w_up, moe_b_up, moe_w_down, moe_b_down)

    return (x[:T_CTX].reshape(N_CTX_SEQ, CTX_LEN, D), x[T_CTX:].reshape(N_LAT_SEQ, LAT_LEN, D),
            new_k, new_v, s_f[:, None], s_b[:, None])
```

```python
import functools

import numpy as np
import jax
import jax.numpy as jnp
from jax import lax
from jax.experimental import pallas as pl
from jax.experimental.pallas import tpu as pltpu

F32 = jnp.float32
BF16 = jnp.bfloat16
I32 = jnp.int32
HIGHEST = lax.Precision.HIGHEST

D = 1024
N_CTX_SEQ = 32
CTX_LEN = 256
N_LAT_SEQ = 2
LAT_LEN = 4096
T_CTX = N_CTX_SEQ * CTX_LEN
T_LAT = N_LAT_SEQ * LAT_LEN
T_ALL = T_CTX + T_LAT
GRID_W = 64
GRID_ROWS = LAT_LEN // GRID_W
NA_HEADS = 16
NA_HD = 64
NA_WIN_ROWS = 8
NA_WIN_COLS = 16
GLA_HEADS = 4
GLA_DK = 128
GLA_DV = 256
GLA_RANK = 16
GLA_GATE_NORM = 16.0
GLA_CHUNK = 64
ROPE_THETA = 10000.0
N_EXPERTS = 32
TOP_K = 4
D_FF = 1024
SWIGLU_LIMIT = 7.0
SWIGLU_ALPHA = 1.702
EPS = 1e-6
NEG_INF = -1e30

LANES = 128
TM = 512
GLA_BLOCK = 256
GLA_HG = 4
MOE_TM = 512
MOE_ROWS = T_ALL * TOP_K
MOE_TILES = MOE_ROWS // MOE_TM
MOE_VISITS = MOE_TILES + N_EXPERTS - 1
TD = 512
TC = 256
DMA_GROUP = 8
VMEM_LIMIT = 60 * 1024 * 1024

_NT = (((1,), (1,)), ((), ()))
_TN = (((0,), (0,)), ((), ()))


def _cparams(n_axes, vmem=None):
    return pltpu.CompilerParams(
        dimension_semantics=("arbitrary",) * n_axes,
        vmem_limit_bytes=VMEM_LIMIT if vmem is None else vmem)


def _mod_idx(i, tm):
    return jnp.maximum((i * tm) // LAT_LEN - 1, 0)


def _rms(x, g):
    return x * lax.rsqrt(jnp.mean(x * x, axis=-1, keepdims=True) + EPS) * g


def _norm_mod(x, g, m_ref, shift_row):
    sh = m_ref[0, shift_row:shift_row + 1, :]
    sc = m_ref[0, shift_row + 1:shift_row + 2, :]
    return _rms(x, g) * (1.0 + sc) + sh


def _mod_kernel(c_ref, w_ref, b_ref, o_ref):
    c = c_ref[...]
    s = c * jax.nn.sigmoid(c)
    o_ref[0] = jnp.dot(s, w_ref[0], precision=HIGHEST, preferred_element_type=F32) + b_ref[0]


def _modulation(cond, w_ada, b_ada):
    depth = w_ada.shape[0]
    out = pl.pallas_call(
        _mod_kernel,
        out_shape=jax.ShapeDtypeStruct((depth, 8, 6 * D), F32),
        grid=(depth, 6),
        in_specs=[pl.BlockSpec((8, D), lambda l, j: (0, 0)),
                  pl.BlockSpec((1, D, D), lambda l, j: (l, 0, j)),
                  pl.BlockSpec((1, 1, D), lambda l, j: (l, 0, j))],
        out_specs=pl.BlockSpec((1, 8, D), lambda l, j: (l, 0, j)),
        compiler_params=_cparams(2),
        name="adaln_modulation",
    )(cond, w_ada, b_ada.reshape(depth, 1, 6 * D))
    return out.reshape(depth, 8, 6, D)


CTX_TILES = T_CTX // TM


def _ctx_part(i):
    return (jnp.minimum(i, CTX_TILES - 1), 0)


def _lat_part(i):
    return (jnp.maximum(i - CTX_TILES, 0), 0)


def _nmm_kernel(*refs, shift_row, split_x, kv_out):
    refs = list(refs)
    i = pl.program_id(0)
    if split_x:
        xc_ref, xl_ref = refs[:2]
        del refs[:2]
        x = jnp.where(i < CTX_TILES, xc_ref[...], xl_ref[...])
    else:
        x = refs.pop(0)[...]
    g_ref, m_ref, w_ref, o_ref = refs[:4]
    del refs[:4]
    if kv_out:
        k_ref, v_ref = refs[:2]
        del refs[:2]
    (wb_ref,) = refs

    @pl.when(i == 0)
    def _():
        wb_ref[...] = w_ref[...].astype(BF16)

    h = _norm_mod(x, g_ref[...], m_ref, shift_row)
    res = jnp.dot(h.astype(BF16), wb_ref[...], preferred_element_type=F32)
    o_ref[...] = res
    if kv_out:
        @pl.when(i < CTX_TILES)
        def _():
            k_ref[...] = res[:, D:2 * D]
            v_ref[...] = res[:, 2 * D:]


def _norm_mod_matmul(x, g, mod, w, shift_row, kv_out=False):
    split_x = isinstance(x, tuple)
    n = w.shape[1]
    if split_x:
        x_specs = [pl.BlockSpec((TM, D), _ctx_part), pl.BlockSpec((TM, D), _lat_part)]
        x_args = list(x)
    else:
        x_specs = [pl.BlockSpec((TM, D), lambda i: (i, 0))]
        x_args = [x]
    out_shape = [jax.ShapeDtypeStruct((T_ALL, n), F32)]
    out_specs = [pl.BlockSpec((TM, n), lambda i: (i, 0))]
    if kv_out:
        out_shape += [jax.ShapeDtypeStruct((T_CTX, D), F32)] * 2
        out_specs += [pl.BlockSpec((TM, D), _ctx_part)] * 2
    res = pl.pallas_call(
        functools.partial(_nmm_kernel, shift_row=shift_row, split_x=split_x, kv_out=kv_out),
        out_shape=out_shape,
        grid=(T_ALL // TM,),
        in_specs=x_specs + [pl.BlockSpec((1, D), lambda i: (0, 0)),
                            pl.BlockSpec((1, 6, D), lambda i: (_mod_idx(i, TM), 0, 0)),
                            pl.BlockSpec((D, n), lambda i: (0, 0), pipeline_mode=pl.Buffered(1))],
        out_specs=out_specs,
        scratch_shapes=[pltpu.VMEM((D, n), BF16)],
        compiler_params=_cparams(1),
        name="norm_mod_matmul",
    )(*x_args, g.reshape(1, D), mod, w)
    return res if kv_out else res[0]


def _ctx_attn_kernel(qkv_ref, o_ref):
    lane = lax.broadcasted_iota(I32, (1, LANES), 1)
    scale = NA_HD ** -0.5
    for hp in range(NA_HEADS // 2):
        q = qkv_ref[0, :, hp * LANES:(hp + 1) * LANES] * scale
        k = qkv_ref[0, :, D + hp * LANES:D + (hp + 1) * LANES].astype(BF16)
        v = qkv_ref[0, :, 2 * D + hp * LANES:2 * D + (hp + 1) * LANES]
        acc = jnp.zeros((CTX_LEN, LANES), F32)
        for half in range(2):
            msk = (lane < NA_HD) if half == 0 else (lane >= NA_HD)
            qm = jnp.where(msk, q, 0.0).astype(BF16)
            s = lax.dot_general(qm, k, _NT, preferred_element_type=F32)
            p = jnp.exp(s - jnp.max(s, axis=-1, keepdims=True))
            l = jnp.sum(p, axis=-1, keepdims=True)
            vm = jnp.where(msk, v, 0.0).astype(BF16)
            acc = acc + jnp.dot(p.astype(BF16), vm, preferred_element_type=F32) / l
        o_ref[0, :, hp * LANES:(hp + 1) * LANES] = acc


def _ctx_attention(qkv):
    out = pl.pallas_call(
        _ctx_attn_kernel,
        out_shape=jax.ShapeDtypeStruct((N_CTX_SEQ, CTX_LEN, D), F32),
        grid=(N_CTX_SEQ,),
        in_specs=[pl.BlockSpec((1, CTX_LEN, 3 * D), lambda b: (b, 0, 0))],
        out_specs=pl.BlockSpec((1, CTX_LEN, D), lambda b: (b, 0, 0)),
        compiler_params=_cparams(1),
        name="context_attention",
    )(qkv.reshape(T_ALL // CTX_LEN, CTX_LEN, 3 * D))
    return out.reshape(T_CTX, D)


NA_QR = 8
NA_KR = 2 * NA_QR
NA_NBLK = GRID_ROWS // NA_QR
NA_N_OFF = 2 * NA_WIN_ROWS - 1
NA_SLAB_BOTH, NA_SLAB_LEFT, NA_SLAB_RIGHT, NA_SLAB_NONE = 0, NA_N_OFF - 1, 2 * NA_N_OFF - 1, 3 * NA_N_OFF - 1
NA_NSLAB = 3 * NA_N_OFF


def _na_slab_ids():
    ids = np.zeros((3, NA_QR, NA_KR // 2), np.int64)
    half = NA_WIN_ROWS // 2
    starts = ((0, 0), (NA_QR, NA_QR - half), (GRID_ROWS - NA_QR, GRID_ROWS - NA_KR))
    for v, (rb, kb) in enumerate(starts):
        for qr in range(NA_QR):
            qa = rb + qr
            r0 = min(max(qa - half, 0), GRID_ROWS - NA_WIN_ROWS)
            for m in range(NA_KR // 2):
                kl, kr = kb + 2 * m, kb + 2 * m + 1
                vl, vr = r0 <= kl < r0 + NA_WIN_ROWS, r0 <= kr < r0 + NA_WIN_ROWS
                ol, orr = kl - qa + NA_WIN_ROWS - 1, kr - qa + NA_WIN_ROWS - 1
                if vl and vr:
                    ids[v, qr, m] = NA_SLAB_BOTH + ol
                elif vl:
                    ids[v, qr, m] = NA_SLAB_LEFT + ol
                elif vr:
                    ids[v, qr, m] = NA_SLAB_RIGHT + orr
                else:
                    ids[v, qr, m] = NA_SLAB_NONE
    return ids


def _na_kernel(q_ref, k_ref, v_ref, ck_ref, cv_ref, slab_ref, o_ref, bias_scr):
    slab_ids = _na_slab_ids()
    for half in range(2):
        for v in range(3):
            for qr in range(NA_QR):
                for m in range(NA_KR // 2):
                    bias_scr[half, v, qr * GRID_W:(qr + 1) * GRID_W, m * LANES:(m + 1) * LANES] = (
                        slab_ref[half, int(slab_ids[v, qr, m])])

    lane = lax.broadcasted_iota(I32, (1, LANES), 1)
    masks = ((lane < NA_HD), (lane >= NA_HD))
    scale = NA_HD ** -0.5
    ck = ck_ref[0].astype(BF16)
    cv = cv_ref[0]
    cvm = [jnp.where(m, cv, 0.0).astype(BF16) for m in masks]
    nq = NA_QR * GRID_W
    nk = NA_KR * GRID_W

    def body(blk, carry):
        rb = blk * NA_QR
        kb = jnp.clip(rb - NA_WIN_ROWS // 2, 0, GRID_ROWS - NA_KR)
        layout = jnp.where(blk == 0, 0, jnp.where(blk == NA_NBLK - 1, 2, 1))
        qs = pl.ds(pl.multiple_of(rb * GRID_W, nq), nq)
        ws = pl.ds(pl.multiple_of(kb * GRID_W, GRID_W * NA_WIN_ROWS // 2), nk)
        q = q_ref[0, qs, :] * scale
        kw = k_ref[0, ws, :].astype(BF16)
        vw = v_ref[0, ws, :]
        halves = range(2)
        qms = [jnp.where(masks[h], q, 0.0).astype(BF16) for h in halves]
        sws = [lax.dot_general(qms[h], kw, _NT, preferred_element_type=F32) + bias_scr[h, pl.ds(layout, 1)][0]
               for h in halves]
        scs = [lax.dot_general(qms[h], ck, _NT, preferred_element_type=F32) for h in halves]
        ms = [jnp.maximum(jnp.max(sws[h], axis=-1, keepdims=True), jnp.max(scs[h], axis=-1, keepdims=True))
              for h in halves]
        pws = [jnp.exp(sws[h] - ms[h]) for h in halves]
        pcs = [jnp.exp(scs[h] - ms[h]) for h in halves]
        ls = [jnp.sum(pws[h], axis=-1, keepdims=True) + jnp.sum(pcs[h], axis=-1, keepdims=True) for h in halves]
        vms = [jnp.where(masks[h], vw, 0.0).astype(BF16) for h in halves]
        os_ = [jnp.dot(pws[h].astype(BF16), vms[h], preferred_element_type=F32)
               + jnp.dot(pcs[h].astype(BF16), cvm[h], preferred_element_type=F32) for h in halves]
        o_ref[0, qs, :] = os_[0] / ls[0] + os_[1] / ls[1]
        return carry

    lax.fori_loop(0, NA_NBLK, body, 0)


def _na_slab_table(rpb):
    cidx = np.arange(GRID_W)
    col_start = np.clip(cidx - NA_WIN_COLS // 2, 0, GRID_W - NA_WIN_COLS)
    col_ok = (cidx[None, :] >= col_start[:, None]) & (cidx[None, :] < col_start[:, None] + NA_WIN_COLS)
    coff = np.clip(cidx[None, :] - cidx[:, None], -(NA_WIN_COLS - 1), NA_WIN_COLS - 1) + NA_WIN_COLS - 1
    n_coff = 2 * NA_WIN_COLS - 1
    onehot = (coff[None, :, :] == np.arange(n_coff)[:, None, None]).astype(np.float32)
    tab = jnp.einsum("hrj,jcw->hrcw", rpb.astype(F32), onehot, precision=HIGHEST)
    tab = jnp.where(col_ok[None, None], tab, NEG_INF)
    neg = jnp.full_like(tab, NEG_INF)
    both = jnp.concatenate([tab[:, :-1], tab[:, 1:]], axis=-1)
    left = jnp.concatenate([tab, neg], axis=-1)
    right = jnp.concatenate([neg, tab], axis=-1)
    none = jnp.concatenate([neg[:, :1], neg[:, :1]], axis=-1)
    return jnp.concatenate([both, left, right, none], axis=1)


def _na_attention(qkv, cache_k, cache_v, rpb):
    off = T_CTX // LAT_LEN
    qkv3 = qkv.reshape(T_ALL // LAT_LEN, LAT_LEN, 3 * D)
    ck = cache_k.reshape(N_LAT_SEQ, CTX_LEN, D)
    cv = cache_v.reshape(N_LAT_SEQ, CTX_LEN, D)
    slabs = _na_slab_table(rpb)
    nh = D // LANES
    out = pl.pallas_call(
        _na_kernel,
        out_shape=jax.ShapeDtypeStruct((N_LAT_SEQ, LAT_LEN, D), F32),
        grid=(N_LAT_SEQ, nh),
        in_specs=[pl.BlockSpec((1, LAT_LEN, LANES), lambda b, h: (b + off, 0, h)),
                  pl.BlockSpec((1, LAT_LEN, LANES), lambda b, h: (b + off, 0, nh + h)),
                  pl.BlockSpec((1, LAT_LEN, LANES), lambda b, h: (b + off, 0, 2 * nh + h)),
                  pl.BlockSpec((1, CTX_LEN, LANES), lambda b, h: (b, 0, h)),
                  pl.BlockSpec((1, CTX_LEN, LANES), lambda b, h: (b, 0, h)),
                  pl.BlockSpec((2, NA_NSLAB, GRID_W, 2 * GRID_W), lambda b, h: (h, 0, 0, 0))],
        out_specs=pl.BlockSpec((1, LAT_LEN, LANES), lambda b, h: (b, 0, h)),
        scratch_shapes=[pltpu.VMEM((2, 3, NA_QR * GRID_W, NA_KR * GRID_W), F32)],
        compiler_params=_cparams(2),
        name="neighbourhood_attention",
    )(qkv3, qkv3, qkv3, ck, cv, slabs)
    return out.reshape(T_LAT, D)


def _proj_res_kernel(ac_ref, al_ref, xc_ref, xl_ref, m_ref, g_ref, w_ref, o_ref, wb_ref, *, gate_row):
    i = pl.program_id(0)

    @pl.when(i == 0)
    def _():
        wb_ref[...] = w_ref[...].astype(BF16)

    a = jnp.where(i < CTX_TILES, ac_ref[...], al_ref[...])
    y = jnp.dot(a.astype(BF16), wb_ref[...], preferred_element_type=F32)
    x = jnp.where(i < CTX_TILES, xc_ref[...], xl_ref[...])
    o_ref[...] = x + m_ref[0, gate_row:gate_row + 1, :] * _rms(y, g_ref[...])


def _proj_residual(a_ctx, a_lat, x_ctx, x_lat, mod, g_post, w, gate_row):
    return pl.pallas_call(
        functools.partial(_proj_res_kernel, gate_row=gate_row),
        out_shape=jax.ShapeDtypeStruct((T_ALL, D), F32),
        grid=(T_ALL // TM,),
        in_specs=[pl.BlockSpec((TM, D), _ctx_part),
                  pl.BlockSpec((TM, D), _lat_part),
                  pl.BlockSpec((TM, D), _ctx_part),
                  pl.BlockSpec((TM, D), _lat_part),
                  pl.BlockSpec((1, 6, D), lambda i: (_mod_idx(i, TM), 0, 0)),
                  pl.BlockSpec((1, D), lambda i: (0, 0)),
                  pl.BlockSpec((D, D), lambda i: (0, 0))],
        out_specs=pl.BlockSpec((TM, D), lambda i: (i, 0)),
        scratch_shapes=[pltpu.VMEM((D, D), BF16)],
        compiler_params=_cparams(1),
        name="proj_residual",
    )(a_ctx, a_lat, x_ctx, x_lat, mod, g_post.reshape(1, D), w)


def _gla_gate_kernel(x_ref, g_ref, m_ref, w1_ref, w2_ref, b_ref, o_ref):
    h = _norm_mod(x_ref[...], g_ref[...], m_ref, 0)
    z = jnp.dot(h.astype(BF16), w1_ref[...].astype(BF16), preferred_element_type=F32)
    y = jnp.dot(z, w2_ref[...], precision=HIGHEST, preferred_element_type=F32) + b_ref[...]
    o_ref[...] = (jnp.minimum(y, 0.0) - jnp.log(1.0 + jnp.exp(-jnp.abs(y)))) * (1.0 / GLA_GATE_NORM)


def _gla_gates(x, g, mod, w1f, w2f, bf, w1b, w2b, bb):
    hk = GLA_HEADS * GLA_DK
    w1 = jnp.zeros((D, LANES), F32).at[:, :GLA_RANK].set(w1f).at[:, GLA_RANK:2 * GLA_RANK].set(w1b)
    w2 = jnp.zeros((LANES, 2 * hk), F32).at[:GLA_RANK, :hk].set(w2f).at[GLA_RANK:2 * GLA_RANK, hk:].set(w2b)
    b = jnp.concatenate([bf, bb]).reshape(1, 2 * hk)
    return pl.pallas_call(
        _gla_gate_kernel,
        out_shape=jax.ShapeDtypeStruct((T_ALL, 2 * hk), F32),
        grid=(T_ALL // TM,),
        in_specs=[pl.BlockSpec((TM, D), lambda i: (i, 0)),
                  pl.BlockSpec((1, D), lambda i: (0, 0)),
                  pl.BlockSpec((1, 6, D), lambda i: (_mod_idx(i, TM), 0, 0)),
                  pl.BlockSpec((D, LANES), lambda i: (0, 0)),
                  pl.BlockSpec((LANES, 2 * hk), lambda i: (0, 0)),
                  pl.BlockSpec((1, 2 * hk), lambda i: (0, 0))],
        out_specs=pl.BlockSpec((TM, 2 * hk), lambda i: (i, 0)),
        compiler_params=_cparams(1),
        name="gla_gates",
    )(x, g.reshape(1, D), mod, w1, w2, b)


def _rope(x, cos, sin_signed):
    lane = lax.broadcasted_iota(I32, (1, LANES), 1)
    partner = jnp.where((lane % 64) < 32, pltpu.roll(x, LANES - 32, 1), pltpu.roll(x, 32, 1))
    return x * cos + partner * sin_signed


def _gla_chunks(chunks, states):
    L = GLA_CHUNK
    cums = [jnp.dot(c["tri"].astype(F32), c["g"], precision=HIGHEST, preferred_element_type=F32) for c in chunks]
    cls = [cum[L - 1:L, :] if c["forward"] else cum[0:1, :] for c, cum in zip(chunks, cums)]
    q_decs = [(c["q"] * (GLA_DK ** -0.5) * jnp.exp(cum)).astype(BF16) for c, cum in zip(chunks, cums)]
    k_decs = [(c["k"] * jnp.exp(-cum)).astype(BF16) for c, cum in zip(chunks, cums)]
    k_rems = [(c["k"] * jnp.exp(cl - cum)).astype(BF16) for c, cum, cl in zip(chunks, cums, cls)]
    vbs = [c["v"].astype(BF16) for c in chunks]
    kv_ts = [lax.dot_general(vb, k_rem, _TN, preferred_element_type=F32) for vb, k_rem in zip(vbs, k_rems)]
    a_s = [lax.dot_general(q_dec, k_dec, _NT, preferred_element_type=F32) for q_dec, k_dec in zip(q_decs, k_decs)]
    a_s = [jnp.where(c["tri"], a, 0.0).astype(BF16) for c, a in zip(chunks, a_s)]
    outs = [jnp.dot(a, vb, preferred_element_type=F32) for a, vb in zip(a_s, vbs)]
    states = dict(states)
    for i, c in enumerate(chunks):
        st = states[c["scan"]]
        outs[i] = outs[i] + lax.dot_general(q_decs[i], st.astype(BF16), _NT, preferred_element_type=F32)
        states[c["scan"]] = jnp.exp(cls[i]) * st + kv_ts[i]
    return outs, states


def _gla_kernel(*refs, rotary, has_init, out_state, nblk):
    refs = list(refs)
    qf, kf, vf, gf, qb, kb, vb, gb = refs[:8]
    del refs[:8]
    if rotary:
        cosf, sinf, cosb, sinb = refs[:4]
        del refs[:4]
    if has_init:
        sf0, sb0 = refs[:2]
        del refs[:2]
    of, ob = refs[:2]
    del refs[:2]
    if out_state:
        sfo, sbo = refs[:2]
        del refs[:2]
    st_f, st_b = refs

    j = pl.program_id(2)

    @pl.when(j == 0)
    def _():
        if has_init:
            for hh in range(GLA_HG):
                st_f[hh] = sf0[0, hh].T
                st_b[hh] = sb0[0, hh].T
        else:
            st_f[...] = jnp.zeros_like(st_f)
            st_b[...] = jnp.zeros_like(st_b)

    L = GLA_CHUNK
    row = lax.broadcasted_iota(I32, (L, L), 0)
    col = lax.broadcasted_iota(I32, (L, L), 1)
    tri_f = col <= row
    tri_b = col >= row
    nchunk = GLA_BLOCK // L

    chunks, where, states = [], [], {}
    for hh in range(GLA_HG):
        ks = slice(hh * GLA_DK, (hh + 1) * GLA_DK)
        vs = slice(hh * GLA_DV, (hh + 1) * GLA_DV)
        states[(hh, True)] = st_f[hh]
        states[(hh, False)] = st_b[hh]
        for forward in (True, False):
            q_ref, k_ref, v_ref, g_ref = (qf, kf, vf, gf) if forward else (qb, kb, vb, gb)
            for c in (range(nchunk) if forward else reversed(range(nchunk))):
                sl = slice(c * L, (c + 1) * L)
                q, k = q_ref[0, sl, ks], k_ref[0, sl, ks]
                if rotary:
                    cos, sin = (cosf, sinf) if forward else (cosb, sinb)
                    q = _rope(q, cos[sl, :], sin[sl, :])
                    k = _rope(k, cos[sl, :], sin[sl, :])
                chunks.append(dict(q=q, k=k, v=v_ref[0, sl, vs], g=g_ref[0, sl, ks], forward=forward,
                                   tri=tri_f if forward else tri_b, scan=(hh, forward)))
                where.append((of if forward else ob, sl, vs))
    outs, states = _gla_chunks(chunks, states)
    for (o_ref, sl, vs), o in zip(where, outs):
        o_ref[0, sl, vs] = o
    for hh in range(GLA_HG):
        st_f[hh] = states[(hh, True)]
        st_b[hh] = states[(hh, False)]

    if out_state:
        @pl.when(j == nblk - 1)
        def _():
            for hh in range(GLA_HG):
                sfo[0, hh] = st_f[hh].T
                sbo[0, hh] = st_b[hh].T


def _rope_tables(n):
    t = np.arange(n)
    n_freq = GLA_DK // 4
    inv = ROPE_THETA ** (-np.arange(n_freq, dtype=np.float64) / n_freq)
    ang_r = (t // GRID_W).astype(np.float64)[:, None] * inv[None, :]
    ang_c = (t % GRID_W).astype(np.float64)[:, None] * inv[None, :]
    cos = np.concatenate([np.cos(ang_r), np.cos(ang_r), np.cos(ang_c), np.cos(ang_c)], axis=1)
    sin = np.concatenate([-np.sin(ang_r), np.sin(ang_r), -np.sin(ang_c), np.sin(ang_c)], axis=1)
    return jnp.asarray(cos, F32), jnp.asarray(sin, F32)


def _gla_scan(proj, gates, *, seq_len, n_seq, seq_off, s_f0=None, s_b0=None, rotary=False, out_state=False):
    nseq_all = T_ALL // seq_len
    nblk = seq_len // GLA_BLOCK
    nh = GLA_HEADS
    proj3 = proj.reshape(nseq_all, seq_len, 3 * D)
    g3 = gates.reshape(nseq_all, seq_len, 2 * nh * GLA_DK)
    has_init = s_f0 is not None

    def fwd(c0):
        return lambda b, h, j: (b + seq_off, j, c0 + h)

    def bwd(c0):
        return lambda b, h, j: (b + seq_off, nblk - 1 - j, c0 + h)

    ngrp = nh // GLA_HG
    qk = (1, GLA_BLOCK, GLA_HG * GLA_DK)
    vv = (1, GLA_BLOCK, GLA_HG * GLA_DV)
    in_specs = [pl.BlockSpec(qk, fwd(0)), pl.BlockSpec(qk, fwd(ngrp)), pl.BlockSpec(vv, fwd(ngrp)), pl.BlockSpec(qk, fwd(0)),
                pl.BlockSpec(qk, bwd(0)), pl.BlockSpec(qk, bwd(ngrp)), pl.BlockSpec(vv, bwd(ngrp)), pl.BlockSpec(qk, bwd(ngrp))]
    args = [proj3, proj3, proj3, g3, proj3, proj3, proj3, g3]
    if rotary:
        cos, sin = _rope_tables(seq_len)
        tab = (GLA_BLOCK, GLA_DK)
        in_specs += [pl.BlockSpec(tab, lambda b, h, j: (j, 0)), pl.BlockSpec(tab, lambda b, h, j: (j, 0)),
                     pl.BlockSpec(tab, lambda b, h, j: (nblk - 1 - j, 0)), pl.BlockSpec(tab, lambda b, h, j: (nblk - 1 - j, 0))]
        args += [cos, sin, cos, sin]
    st = (1, GLA_HG, GLA_DK, GLA_DV)
    if has_init:
        in_specs += [pl.BlockSpec(st, lambda b, h, j: (b, h, 0, 0))] * 2
        args += [s_f0, s_b0]
    out_shape = [jax.ShapeDtypeStruct((n_seq, seq_len, D), F32)] * 2
    out_specs = [pl.BlockSpec(vv, lambda b, h, j: (b, j, h)),
                 pl.BlockSpec(vv, lambda b, h, j: (b, nblk - 1 - j, h))]
    if out_state:
        out_shape += [jax.ShapeDtypeStruct((n_seq, nh, GLA_DK, GLA_DV), F32)] * 2
        out_specs += [pl.BlockSpec(st, lambda b, h, j: (b, h, 0, 0))] * 2
    res = pl.pallas_call(
        functools.partial(_gla_kernel, rotary=rotary, has_init=has_init, out_state=out_state, nblk=nblk),
        out_shape=out_shape,
        grid=(n_seq, ngrp, nblk),
        in_specs=in_specs,
        out_specs=out_specs,
        scratch_shapes=[pltpu.VMEM((GLA_HG, GLA_DV, GLA_DK), F32)] * 2,
        compiler_params=_cparams(3),
        name="gla_scan_rope" if rotary else "gla_scan",
    )(*args)
    of, ob = res[0].reshape(n_seq * seq_len, D), res[1].reshape(n_seq * seq_len, D)
    if out_state:
        return of, ob, res[2], res[3]
    return of, ob


def _gla_out_kernel(ofc_ref, obc_ref, ofl_ref, obl_ref, r_ref, ng_ref, x_ref, m_ref, g_ref, w_ref, o_ref, wb_ref):
    i = pl.program_id(0)

    @pl.when(i == 0)
    def _():
        wb_ref[...] = w_ref[...].astype(BF16)

    o = jnp.where(i < CTX_TILES, ofc_ref[...] + obc_ref[...], ofl_ref[...] + obl_ref[...])
    r = r_ref[...]
    ng = ng_ref[...]
    parts = []
    for h in range(GLA_HEADS):
        oh = o[:, h * GLA_DV:(h + 1) * GLA_DV]
        parts.append(_rms(oh, ng))
    a = jnp.concatenate(parts, axis=1) * (r * jax.nn.sigmoid(r))
    y = jnp.dot(a.astype(BF16), wb_ref[...], preferred_element_type=F32)
    o_ref[...] = x_ref[...] + m_ref[0, 2:3, :] * _rms(y, g_ref[...])


def _gla_output(of_ctx, ob_ctx, of_lat, ob_lat, proj, norm_g, x, mod, g_post, w):
    return pl.pallas_call(
        _gla_out_kernel,
        out_shape=jax.ShapeDtypeStruct((T_ALL, D), F32),
        grid=(T_ALL // TM,),
        in_specs=[pl.BlockSpec((TM, D), _ctx_part),
                  pl.BlockSpec((TM, D), _ctx_part),
                  pl.BlockSpec((TM, D), _lat_part),
                  pl.BlockSpec((TM, D), _lat_part),
                  pl.BlockSpec((TM, D), lambda i: (i, 2)),
                  pl.BlockSpec((1, GLA_DV), lambda i: (0, 0)),
                  pl.BlockSpec((TM, D), lambda i: (i, 0)),
                  pl.BlockSpec((1, 6, D), lambda i: (_mod_idx(i, TM), 0, 0)),
                  pl.BlockSpec((1, D), lambda i: (0, 0)),
                  pl.BlockSpec((D, D), lambda i: (0, 0))],
        out_specs=pl.BlockSpec((TM, D), lambda i: (i, 0)),
        scratch_shapes=[pltpu.VMEM((D, D), BF16)],
        compiler_params=_cparams(1),
        name="gla_output",
    )(of_ctx, ob_ctx, of_lat, ob_lat, proj, norm_g.reshape(1, GLA_DV), x, mod, g_post.reshape(1, D), w)


def _router_kernel(x_ref, g_ref, m_ref, wr_ref, br_ref, h_ref, ri_ref, rw_ref, cnt_ref, cnt_scr):
    @pl.when(pl.program_id(0) == 0)
    def _():
        cnt_scr[...] = jnp.zeros_like(cnt_scr)

    h = _norm_mod(x_ref[...], g_ref[...], m_ref, 3)
    h_ref[...] = h
    logits = jnp.dot(h, wr_ref[...], precision=HIGHEST, preferred_element_type=F32) + br_ref[...]
    lane = lax.broadcasted_iota(I32, (TM, LANES), 1)
    lane_f = lane.astype(F32)
    cur = jnp.where(lane < N_EXPERTS, logits, -jnp.inf)
    vals, sels = [], []
    hot = jnp.zeros((TM, LANES), F32)
    for _ in range(TOP_K):
        m = jnp.max(cur, axis=-1, keepdims=True)
        idx = jnp.min(jnp.where(cur == m, lane_f, float(LANES)), axis=-1, keepdims=True)
        sel = lane_f == idx
        vals.append(m)
        sels.append((idx, sel))
        hot = hot + sel.astype(F32)
        cur = jnp.where(sel, -jnp.inf, cur)
    ex = [jnp.exp(v - vals[0]) for v in vals]
    den = ex[0] + ex[1] + ex[2] + ex[3]
    r_i = lax.broadcasted_iota(I32, (TM, TM), 0)
    c_i = lax.broadcasted_iota(I32, (TM, TM), 1)
    before = (c_i < r_i).astype(BF16)
    prefix = jnp.dot(before, hot.astype(BF16), preferred_element_type=F32) + cnt_scr[0:1, :]
    ri = jnp.zeros((TM, LANES), F32)
    rw = jnp.zeros((TM, LANES), F32)
    for k in range(TOP_K):
        idx, sel = sels[k]
        rank = jnp.sum(jnp.where(sel, prefix, 0.0), axis=-1, keepdims=True)
        ri = jnp.where(lane == k, idx, ri)
        ri = jnp.where(lane == TOP_K + k, rank, ri)
        rw = jnp.where(lane == k, ex[k] / den, rw)
    ri_ref[...] = ri.T[0:2 * TOP_K, :].astype(I32)
    rw_ref[...] = rw
    cnt = cnt_scr[...] + jnp.sum(hot, axis=0, keepdims=True)
    cnt_scr[...] = cnt
    cnt_ref[...] = cnt


def _router(x, g, mod, w_router, b_router):
    wr = jnp.zeros((D, LANES), F32).at[:, :N_EXPERTS].set(w_router)
    br = jnp.zeros((1, LANES), F32).at[0, :N_EXPERTS].set(b_router)
    return pl.pallas_call(
        _router_kernel,
        out_shape=[jax.ShapeDtypeStruct((T_ALL, D), F32),
                   jax.ShapeDtypeStruct((2 * TOP_K, T_ALL), I32),
                   jax.ShapeDtypeStruct((T_ALL, LANES), F32),
                   jax.ShapeDtypeStruct((8, LANES), F32)],
        grid=(T_ALL // TM,),
        in_specs=[pl.BlockSpec((TM, D), lambda i: (i, 0)),
                  pl.BlockSpec((1, D), lambda i: (0, 0)),
                  pl.BlockSpec((1, 6, D), lambda i: (_mod_idx(i, TM), 0, 0)),
                  pl.BlockSpec((D, LANES), lambda i: (0, 0)),
                  pl.BlockSpec((1, LANES), lambda i: (0, 0))],
        out_specs=[pl.BlockSpec((TM, D), lambda i: (i, 0)),
                   pl.BlockSpec((2 * TOP_K, TM), lambda i: (0, i)),
                   pl.BlockSpec((TM, LANES), lambda i: (i, 0)),
                   pl.BlockSpec((8, LANES), lambda i: (0, 0))],
        scratch_shapes=[pltpu.VMEM((8, LANES), F32)],
        compiler_params=_cparams(1),
        name="moe_router",
    )(x, g.reshape(1, D), mod, wr, br)


def _dispatch_kernel(dest_ref, h_ref, xs_ref, sem):
    base = pl.program_id(0) * TD

    def row_copy(j, u, k):
        i0 = pl.multiple_of(j * DMA_GROUP, DMA_GROUP)
        d = dest_ref[k * T_ALL + base + i0 + u]
        return pltpu.make_async_copy(h_ref.at[pl.ds(i0, DMA_GROUP)].at[pl.ds(u, 1)], xs_ref.at[pl.ds(d, 1)], sem)

    def start(j, c):
        for u in range(DMA_GROUP):
            for k in range(TOP_K):
                row_copy(j, u, k).start(priority=k % 2)
        return c

    def wait(j, c):
        for u in range(DMA_GROUP):
            for k in range(TOP_K):
                row_copy(j, u, k).wait()
        return c

    lax.fori_loop(0, TD // DMA_GROUP, start, 0)
    lax.fori_loop(0, TD // DMA_GROUP, wait, 0)


def _dispatch(dest, h):
    return pl.pallas_call(
        _dispatch_kernel,
        out_shape=jax.ShapeDtypeStruct((MOE_ROWS, D), F32),
        grid_spec=pltpu.PrefetchScalarGridSpec(
            num_scalar_prefetch=1,
            grid=(T_ALL // TD,),
            in_specs=[pl.BlockSpec((TD, D), lambda i, dest: (i, 0))],
            out_specs=pl.BlockSpec(memory_space=pl.ANY),
            scratch_shapes=[pltpu.SemaphoreType.DMA(())]),
        compiler_params=_cparams(1),
        name="moe_dispatch",
    )(dest, h)


def _gmm_kernel(vt_ref, vg_ref, lo_ref, hi_ref, first_ref, x_ref, wu_ref, bu_ref, wd_ref, bd_ref, y_ref, wub, wdb):
    v = pl.program_id(0)
    g = vg_ref[v]
    new_group = jnp.logical_or(v == 0, vg_ref[jnp.maximum(v - 1, 0)] != g)

    @pl.when(new_group)
    def _():
        wub[...] = wu_ref[0, 0].astype(BF16)
        wdb[...] = wd_ref[0, 0].astype(BF16)

    lo = lo_ref[v]
    hi = hi_ref[v]

    def expert_rows():
        u = jnp.dot(x_ref[...].astype(BF16), wub[...], preferred_element_type=F32) + bu_ref[0, 0]
        gate = jnp.minimum(u[:, :D_FF], SWIGLU_LIMIT)
        lin = jnp.clip(u[:, D_FF:], -SWIGLU_LIMIT, SWIGLU_LIMIT)
        act = gate * jax.nn.sigmoid(SWIGLU_ALPHA * gate) * (lin + 1.0)
        y = jnp.dot(act.astype(BF16), wdb[...], preferred_element_type=F32) + bd_ref[0, 0]
        row = lax.broadcasted_iota(I32, (MOE_TM, 1), 0)
        return y, jnp.logical_and(row >= lo, row < hi)

    @pl.when(jnp.logical_and(hi > lo, first_ref[v] == 1))
    def _():
        y, mine = expert_rows()
        y_ref[...] = jnp.where(mine, y, 0.0)

    @pl.when(jnp.logical_and(hi > lo, first_ref[v] == 0))
    def _():
        y, mine = expert_rows()
        y_ref[...] = jnp.where(mine, y, y_ref[...])


def _moe_visits(counts):
    ends = jnp.cumsum(counts)
    starts = ends - counts
    first_tile = starts // MOE_TM
    last_tile = (ends - 1) // MOE_TM
    ntl = jnp.where(counts > 0, last_tile - first_tile + 1, 0)
    vend = jnp.cumsum(ntl)
    vstart = vend - ntl
    total = vend[-1]
    v = jnp.arange(MOE_VISITS, dtype=I32)
    vc = jnp.minimum(v, total - 1)
    grp = jnp.minimum(jnp.sum((vend[None, :] <= vc[:, None]).astype(I32), axis=1), N_EXPERTS - 1)
    tile = first_tile[grp] + (vc - vstart[grp])
    valid = v < total
    lo = jnp.where(valid, jnp.clip(starts[grp] - tile * MOE_TM, 0, MOE_TM), 0)
    hi = jnp.where(valid, jnp.clip(ends[grp] - tile * MOE_TM, 0, MOE_TM), 0)
    prev_tile = jnp.concatenate([jnp.full((1,), -1, I32), tile[:-1]])
    first = jnp.logical_and(valid, tile != prev_tile).astype(I32)
    return tile.astype(I32), grp, lo.astype(I32), hi.astype(I32), first, starts


def _grouped_mlp(visits, xs, layer, w_up, b_up, w_down, b_down):
    tile, grp, lo, hi, first = visits
    depth = w_up.shape[0]
    return pl.pallas_call(
        _gmm_kernel,
        out_shape=jax.ShapeDtypeStruct((MOE_ROWS, D), F32),
        grid_spec=pltpu.PrefetchScalarGridSpec(
            num_scalar_prefetch=5,
            grid=(MOE_VISITS,),
            in_specs=[pl.BlockSpec((MOE_TM, D), lambda v, vt, vg, lo, hi, fi: (vt[v], 0)),
                      pl.BlockSpec((1, 1, D, 2 * D_FF), lambda v, vt, vg, lo, hi, fi: (layer, vg[v], 0, 0)),
                      pl.BlockSpec((1, 1, 1, 2 * D_FF), lambda v, vt, vg, lo, hi, fi: (layer, vg[v], 0, 0)),
                      pl.BlockSpec((1, 1, D_FF, D), lambda v, vt, vg, lo, hi, fi: (layer, vg[v], 0, 0)),
                      pl.BlockSpec((1, 1, 1, D), lambda v, vt, vg, lo, hi, fi: (layer, vg[v], 0, 0))],
            out_specs=pl.BlockSpec((MOE_TM, D), lambda v, vt, vg, lo, hi, fi: (vt[v], 0)),
            scratch_shapes=[pltpu.VMEM((D, 2 * D_FF), BF16), pltpu.VMEM((D_FF, D), BF16)]),
        compiler_params=_cparams(1),
        name="moe_grouped_mlp",
    )(tile, grp, lo, hi, first, xs, w_up, b_up.reshape(depth, N_EXPERTS, 1, 2 * D_FF), w_down,
      b_down.reshape(depth, N_EXPERTS, 1, D))


def _combine_kernel(dest_ref, ys_ref, rw_ref, x_ref, m_ref, g_ref, *rest, split):
    if split:
        oc_ref, ol_ref, buf, sem = rest
    else:
        o_ref, buf, sem = rest
    step = pl.program_id(0)
    slot = step % 2

    def row_copy(s, j, u, k):
        i0 = pl.multiple_of(j * DMA_GROUP, DMA_GROUP)
        d = dest_ref[k * T_ALL + s * TC + i0 + u]
        return pltpu.make_async_copy(ys_ref.at[pl.ds(d, 1)], buf.at[s % 2, k, pl.ds(i0, DMA_GROUP)].at[pl.ds(u, 1)],
                                     sem.at[s % 2])

    def issue(s):
        def start(j, c):
            for u in range(DMA_GROUP):
                for k in range(TOP_K):
                    row_copy(s, j, u, k).start(priority=k % 2)
            return c

        lax.fori_loop(0, TC // DMA_GROUP, start, 0)

    @pl.when(step == 0)
    def _():
        issue(step)

    @pl.when(step + 1 < pl.num_programs(0))
    def _():
        issue(step + 1)

    def wait(j, c):
        for u in range(DMA_GROUP):
            for k in range(TOP_K):
                row_copy(step, j, u, k).wait()
        return c

    lax.fori_loop(0, TC // DMA_GROUP, wait, 0)
    w = rw_ref[...]
    f = w[:, 0:1] * buf[slot, 0]
    for k in range(1, TOP_K):
        f = f + w[:, k:k + 1] * buf[slot, k]
    res = x_ref[...] + m_ref[0, 5:6, :] * _rms(f, g_ref[...])
    if split:
        @pl.when(step < T_CTX // TC)
        def _():
            oc_ref[...] = res

        @pl.when(step >= T_CTX // TC)
        def _():
            ol_ref[...] = res
    else:
        o_ref[...] = res


def _combine(dest, ys, rw, x, mod, g_post, split):
    nc = T_CTX // TC
    if split:
        out_shape = [jax.ShapeDtypeStruct((T_CTX, D), F32), jax.ShapeDtypeStruct((T_LAT, D), F32)]
        out_specs = [pl.BlockSpec((TC, D), lambda i, dest: (jnp.minimum(i, nc - 1), 0)),
                     pl.BlockSpec((TC, D), lambda i, dest: (jnp.maximum(i - nc, 0), 0))]
    else:
        out_shape = jax.ShapeDtypeStruct((T_ALL, D), F32)
        out_specs = pl.BlockSpec((TC, D), lambda i, dest: (i, 0))
    return pl.pallas_call(
        functools.partial(_combine_kernel, split=split),
        out_shape=out_shape,
        grid_spec=pltpu.PrefetchScalarGridSpec(
            num_scalar_prefetch=1,
            grid=(T_ALL // TC,),
            in_specs=[pl.BlockSpec(memory_space=pl.ANY),
                      pl.BlockSpec((TC, LANES), lambda i, dest: (i, 0)),
                      pl.BlockSpec((TC, D), lambda i, dest: (i, 0)),
                      pl.BlockSpec((1, 6, D), lambda i, dest: (_mod_idx(i, TC), 0, 0)),
                      pl.BlockSpec((1, D), lambda i, dest: (0, 0))],
            out_specs=out_specs,
            scratch_shapes=[pltpu.VMEM((2, TOP_K, TC, D), F32), pltpu.SemaphoreType.DMA((2,))]),
        compiler_params=_cparams(1),
        name="moe_combine",
    )(dest, ys, rw, x, mod, g_post.reshape(1, D))


def _moe(x, mod, g_pre, g_post, w_router, b_router, layer, w_up, b_up, w_down, b_down, split_out=False):
    h, ri, rw, cnt = _router(x, g_pre, mod, w_router, b_router)
    counts = cnt[0, :N_EXPERTS].astype(I32)
    tile, grp, lo, hi, first, starts = _moe_visits(counts)
    dest = (starts[ri[:TOP_K]] + ri[TOP_K:]).reshape(-1).astype(I32)
    xs = _dispatch(dest, h)
    ys = _grouped_mlp((tile, grp, lo, hi, first), xs, layer, w_up, b_up, w_down, b_down)
    return _combine(dest, ys, rw, x, mod, g_post, split_out)


def kernel(x_prompt, x_sample, cache_k, cache_v, state_fwd, state_bwd, c, c_ctx, w_ada, b_ada, g_pre_mix, g_post_mix, g_pre_ffn, g_post_ffn, na_w_qkv, na_rpb, na_w_out, gla_w_in, gla_w_g1_fwd, gla_w_g2_fwd, gla_b_g_fwd, gla_w_g1_bwd, gla_w_g2_bwd, gla_b_g_bwd, gla_norm_g, gla_w_out, moe_w_router, moe_b_router, moe_w_up, moe_b_up, moe_w_down, moe_b_down):
    x_ctx, x_lat = x_prompt.reshape(T_CTX, D), x_sample.reshape(T_LAT, D)
    cond = jnp.concatenate([c_ctx[None, :], c, jnp.zeros((8 - 1 - N_LAT_SEQ, D), F32)], axis=0)
    mods = _modulation(cond, w_ada, b_ada)

    qkv, new_k, new_v = _norm_mod_matmul((x_ctx, x_lat), g_pre_mix[0], mods[0], na_w_qkv[0], 0, kv_out=True)
    a_ctx = _ctx_attention(qkv)
    a_lat = _na_attention(qkv, cache_k[:, 0], cache_v[:, 0], na_rpb[0])
    x = _proj_residual(a_ctx, a_lat, x_ctx, x_lat, mods[0], g_post_mix[0], na_w_out[0], 2)
    new_k = new_k.reshape(N_CTX_SEQ, 1, CTX_LEN, NA_HEADS, NA_HD)
    new_v = new_v.reshape(N_CTX_SEQ, 1, CTX_LEN, NA_HEADS, NA_HD)
    x = _moe(x, mods[0], g_pre_ffn[0], g_post_ffn[0], moe_w_router[0], moe_b_router[0],
             0, moe_w_up, moe_b_up, moe_w_down, moe_b_down)

    proj = _norm_mod_matmul(x, g_pre_mix[1], mods[1], gla_w_in[0], 0)
    gates = _gla_gates(x, g_pre_mix[1], mods[1], gla_w_g1_fwd[0], gla_w_g2_fwd[0], gla_b_g_fwd[0],
                       gla_w_g1_bwd[0], gla_w_g2_bwd[0], gla_b_g_bwd[0])
    of_c, ob_c, s_f, s_b = _gla_scan(proj, gates, seq_len=CTX_LEN, n_seq=N_CTX_SEQ, seq_off=0, out_state=True)
    of_l, ob_l = _gla_scan(proj, gates, seq_len=LAT_LEN, n_seq=N_LAT_SEQ, seq_off=T_CTX // LAT_LEN,
                           s_f0=state_fwd[:, 0], s_b0=state_bwd[:, 0], rotary=True)
    x = _gla_output(of_c, ob_c, of_l, ob_l, proj, gla_norm_g[0], x, mods[1], g_post_mix[1], gla_w_out[0])
    x = _moe(x, mods[1], g_pre_ffn[1], g_post_ffn[1], moe_w_router[1], moe_b_router[1],
             1, moe_w_up, moe_b_up, moe_w_down, moe_b_down, split_out=True)

    return (x[0].reshape(N_CTX_SEQ, CTX_LEN, D), x[1].reshape(N_LAT_SEQ, LAT_LEN, D),
            new_k, new_v, s_f[:, None], s_b[:, None])
```

```python
import functools

import numpy as np
import jax
import jax.numpy as jnp
from jax import lax
from jax.experimental import pallas as pl
from jax.experimental.pallas import tpu as pltpu

F32 = jnp.float32
BF16 = jnp.bfloat16
I32 = jnp.int32
HIGHEST = lax.Precision.HIGHEST

D = 1024
N_CTX_SEQ = 32
CTX_LEN = 256
N_LAT_SEQ = 2
LAT_LEN = 4096
T_CTX = N_CTX_SEQ * CTX_LEN
T_LAT = N_LAT_SEQ * LAT_LEN
T_ALL = T_CTX + T_LAT
GRID_W = 64
GRID_ROWS = LAT_LEN // GRID_W
NA_HEADS = 16
NA_HD = 64
NA_WIN_ROWS = 8
NA_WIN_COLS = 16
GLA_HEADS = 4
GLA_DK = 128
GLA_DV = 256
GLA_RANK = 16
GLA_GATE_NORM = 16.0
GLA_CHUNK = 64
ROPE_THETA = 10000.0
N_EXPERTS = 32
TOP_K = 4
D_FF = 1024
SWIGLU_LIMIT = 7.0
SWIGLU_ALPHA = 1.702
EPS = 1e-6
NEG_INF = -1e30

LANES = 128
TM = 512
GLA_BLOCK = 256
GLA_HG = 4
MOE_TM = 512
MOE_ROWS = T_ALL * TOP_K
MOE_TILES = MOE_ROWS // MOE_TM
MOE_VISITS = MOE_TILES + N_EXPERTS - 1
TD = 512
TC = 256
DMA_GROUP = 8
VMEM_LIMIT = 60 * 1024 * 1024

_NT = (((1,), (1,)), ((), ()))
_TN = (((0,), (0,)), ((), ()))


def _cparams(n_axes, vmem=None):
    return pltpu.CompilerParams(
        dimension_semantics=("arbitrary",) * n_axes,
        vmem_limit_bytes=VMEM_LIMIT if vmem is None else vmem)


def _mod_idx(i, tm):
    return jnp.maximum((i * tm) // LAT_LEN - 1, 0)


def _rms(x, g):
    return x * lax.rsqrt(jnp.mean(x * x, axis=-1, keepdims=True) + EPS) * g


def _norm_mod(x, g, m_ref, shift_row):
    sh = m_ref[0, shift_row:shift_row + 1, :]
    sc = m_ref[0, shift_row + 1:shift_row + 2, :]
    return _rms(x, g) * (1.0 + sc) + sh


def _mod_kernel(c_ref, w_ref, b_ref, o_ref):
    c = c_ref[...]
    s = c * jax.nn.sigmoid(c)
    o_ref[0] = jnp.dot(s, w_ref[0], precision=HIGHEST, preferred_element_type=F32) + b_ref[0]


def _modulation(cond, w_ada, b_ada):
    depth = w_ada.shape[0]
    out = pl.pallas_call(
        _mod_kernel,
        out_shape=jax.ShapeDtypeStruct((depth, 8, 6 * D), F32),
        grid=(depth, 6),
        in_specs=[pl.BlockSpec((8, D), lambda l, j: (0, 0)),
                  pl.BlockSpec((1, D, D), lambda l, j: (l, 0, j)),
                  pl.BlockSpec((1, 1, D), lambda l, j: (l, 0, j))],
        out_specs=pl.BlockSpec((1, 8, D), lambda l, j: (l, 0, j)),
        compiler_params=_cparams(2),
        name="adaln_modulation",
    )(cond, w_ada, b_ada.reshape(depth, 1, 6 * D))
    return out.reshape(depth, 8, 6, D)


CTX_TILES = T_CTX // TM


def _ctx_part(i):
    return (jnp.minimum(i, CTX_TILES - 1), 0)


def _lat_part(i):
    return (jnp.maximum(i - CTX_TILES, 0), 0)


def _nmm_kernel(*refs, shift_row, split_x, kv_out):
    refs = list(refs)
    i = pl.program_id(0)
    if split_x:
        xc_ref, xl_ref = refs[:2]
        del refs[:2]
        x = jnp.where(i < CTX_TILES, xc_ref[...], xl_ref[...])
    else:
        x = refs.pop(0)[...]
    g_ref, m_ref, w_ref, o_ref = refs[:4]
    del refs[:4]
    if kv_out:
        k_ref, v_ref = refs[:2]
        del refs[:2]
    (wb_ref,) = refs

    @pl.when(i == 0)
    def _():
        wb_ref[...] = w_ref[...].astype(BF16)

    h = _norm_mod(x, g_ref[...], m_ref, shift_row)
    res = jnp.dot(h.astype(BF16), wb_ref[...], preferred_element_type=F32)
    o_ref[...] = res
    if kv_out:
        @pl.when(i < CTX_TILES)
        def _():
            k_ref[...] = res[:, D:2 * D]
            v_ref[...] = res[:, 2 * D:]


def _norm_mod_matmul(x, g, mod, w, shift_row, kv_out=False):
    split_x = isinstance(x, tuple)
    n = w.shape[1]
    if split_x:
        x_specs = [pl.BlockSpec((TM, D), _ctx_part), pl.BlockSpec((TM, D), _lat_part)]
        x_args = list(x)
    else:
        x_specs = [pl.BlockSpec((TM, D), lambda i: (i, 0))]
        x_args = [x]
    out_shape = [jax.ShapeDtypeStruct((T_ALL, n), F32)]
    out_specs = [pl.BlockSpec((TM, n), lambda i: (i, 0))]
    if kv_out:
        out_shape += [jax.ShapeDtypeStruct((T_CTX, D), F32)] * 2
        out_specs += [pl.BlockSpec((TM, D), _ctx_part)] * 2
    res = pl.pallas_call(
        functools.partial(_nmm_kernel, shift_row=shift_row, split_x=split_x, kv_out=kv_out),
        out_shape=out_shape,
        grid=(T_ALL // TM,),
        in_specs=x_specs + [pl.BlockSpec((1, D), lambda i: (0, 0)),
                            pl.BlockSpec((1, 6, D), lambda i: (_mod_idx(i, TM), 0, 0)),
                            pl.BlockSpec((D, n), lambda i: (0, 0), pipeline_mode=pl.Buffered(1))],
        out_specs=out_specs,
        scratch_shapes=[pltpu.VMEM((D, n), BF16)],
        compiler_params=_cparams(1),
        name="norm_mod_matmul",
    )(*x_args, g.reshape(1, D), mod, w)
    return res if kv_out else res[0]


def _ctx_attn_kernel(qkv_ref, o_ref):
    lane = lax.broadcasted_iota(I32, (1, LANES), 1)
    scale = NA_HD ** -0.5
    for hp in range(NA_HEADS // 2):
        q = qkv_ref[0, :, hp * LANES:(hp + 1) * LANES] * scale
        k = qkv_ref[0, :, D + hp * LANES:D + (hp + 1) * LANES].astype(BF16)
        v = qkv_ref[0, :, 2 * D + hp * LANES:2 * D + (hp + 1) * LANES]
        acc = jnp.zeros((CTX_LEN, LANES), F32)
        for half in range(2):
            msk = (lane < NA_HD) if half == 0 else (lane >= NA_HD)
            qm = jnp.where(msk, q, 0.0).astype(BF16)
            s = lax.dot_general(qm, k, _NT, preferred_element_type=F32)
            p = jnp.exp(s - jnp.max(s, axis=-1, keepdims=True))
            l = jnp.sum(p, axis=-1, keepdims=True)
            vm = jnp.where(msk, v, 0.0).astype(BF16)
            acc = acc + jnp.dot(p.astype(BF16), vm, preferred_element_type=F32) / l
        o_ref[0, :, hp * LANES:(hp + 1) * LANES] = acc


def _ctx_attention(qkv):
    out = pl.pallas_call(
        _ctx_attn_kernel,
        out_shape=jax.ShapeDtypeStruct((N_CTX_SEQ, CTX_LEN, D), F32),
        grid=(N_CTX_SEQ,),
        in_specs=[pl.BlockSpec((1, CTX_LEN, 3 * D), lambda b: (b, 0, 0))],
        out_specs=pl.BlockSpec((1, CTX_LEN, D), lambda b: (b, 0, 0)),
        compiler_params=_cparams(1),
        name="context_attention",
    )(qkv.reshape(T_ALL // CTX_LEN, CTX_LEN, 3 * D))
    return out.reshape(T_CTX, D)


NA_QR = 8
NA_KR = 2 * NA_QR
NA_NBLK = GRID_ROWS // NA_QR
NA_N_OFF = 2 * NA_WIN_ROWS - 1
NA_SLAB_BOTH, NA_SLAB_LEFT, NA_SLAB_RIGHT, NA_SLAB_NONE = 0, NA_N_OFF - 1, 2 * NA_N_OFF - 1, 3 * NA_N_OFF - 1
NA_NSLAB = 3 * NA_N_OFF


def _na_slab_ids():
    ids = np.zeros((3, NA_QR, NA_KR // 2), np.int64)
    half = NA_WIN_ROWS // 2
    starts = ((0, 0), (NA_QR, NA_QR - half), (GRID_ROWS - NA_QR, GRID_ROWS - NA_KR))
    for v, (rb, kb) in enumerate(starts):
        for qr in range(NA_QR):
            qa = rb + qr
            r0 = min(max(qa - half, 0), GRID_ROWS - NA_WIN_ROWS)
            for m in range(NA_KR // 2):
                kl, kr = kb + 2 * m, kb + 2 * m + 1
                vl, vr = r0 <= kl < r0 + NA_WIN_ROWS, r0 <= kr < r0 + NA_WIN_ROWS
                ol, orr = kl - qa + NA_WIN_ROWS - 1, kr - qa + NA_WIN_ROWS - 1
                if vl and vr:
                    ids[v, qr, m] = NA_SLAB_BOTH + ol
                elif vl:
                    ids[v, qr, m] = NA_SLAB_LEFT + ol
                elif vr:
                    ids[v, qr, m] = NA_SLAB_RIGHT + orr
                else:
                    ids[v, qr, m] = NA_SLAB_NONE
    return ids


def _na_kernel(q_ref, k_ref, v_ref, ck_ref, cv_ref, slab_ref, o_ref, bias_scr):
    slab_ids = _na_slab_ids()
    for half in range(2):
        for v in range(3):
            for qr in range(NA_QR):
                for m in range(NA_KR // 2):
                    bias_scr[half, v, qr * GRID_W:(qr + 1) * GRID_W, m * LANES:(m + 1) * LANES] = (
                        slab_ref[half, int(slab_ids[v, qr, m])])

    lane = lax.broadcasted_iota(I32, (1, LANES), 1)
    masks = ((lane < NA_HD), (lane >= NA_HD))
    scale = NA_HD ** -0.5
    ck = ck_ref[0].astype(BF16)
    cv = cv_ref[0]
    cvm = [jnp.where(m, cv, 0.0).astype(BF16) for m in masks]
    nq = NA_QR * GRID_W
    nk = NA_KR * GRID_W

    def body(blk, carry):
        rb = blk * NA_QR
        kb = jnp.clip(rb - NA_WIN_ROWS // 2, 0, GRID_ROWS - NA_KR)
        layout = jnp.where(blk == 0, 0, jnp.where(blk == NA_NBLK - 1, 2, 1))
        qs = pl.ds(pl.multiple_of(rb * GRID_W, nq), nq)
        ws = pl.ds(pl.multiple_of(kb * GRID_W, GRID_W * NA_WIN_ROWS // 2), nk)
        q = q_ref[0, qs, :] * scale
        kw = k_ref[0, ws, :].astype(BF16)
        vw = v_ref[0, ws, :]
        halves = range(2)
        qms = [jnp.where(masks[h], q, 0.0).astype(BF16) for h in halves]
        sws = [lax.dot_general(qms[h], kw, _NT, preferred_element_type=F32) + bias_scr[h, pl.ds(layout, 1)][0]
               for h in halves]
        scs = [lax.dot_general(qms[h], ck, _NT, preferred_element_type=F32) for h in halves]
        ms = [jnp.maximum(jnp.max(sws[h], axis=-1, keepdims=True), jnp.max(scs[h], axis=-1, keepdims=True))
              for h in halves]
        pws = [jnp.exp(sws[h] - ms[h]) for h in halves]
        pcs = [jnp.exp(scs[h] - ms[h]) for h in halves]
        ls = [jnp.sum(pws[h], axis=-1, keepdims=True) + jnp.sum(pcs[h], axis=-1, keepdims=True) for h in halves]
        vms = [jnp.where(masks[h], vw, 0.0).astype(BF16) for h in halves]
        os_ = [jnp.dot(pws[h].astype(BF16), vms[h], preferred_element_type=F32)
               + jnp.dot(pcs[h].astype(BF16), cvm[h], preferred_element_type=F32) for h in halves]
        o_ref[0, qs, :] = os_[0] / ls[0] + os_[1] / ls[1]
        return carry

    lax.fori_loop(0, NA_NBLK, body, 0)


def _na_slab_table(rpb):
    cidx = np.arange(GRID_W)
    col_start = np.clip(cidx - NA_WIN_COLS // 2, 0, GRID_W - NA_WIN_COLS)
    col_ok = (cidx[None, :] >= col_start[:, None]) & (cidx[None, :] < col_start[:, None] + NA_WIN_COLS)
    coff = np.clip(cidx[None, :] - cidx[:, None], -(NA_WIN_COLS - 1), NA_WIN_COLS - 1) + NA_WIN_COLS - 1
    n_coff = 2 * NA_WIN_COLS - 1
    onehot = (coff[None, :, :] == np.arange(n_coff)[:, None, None]).astype(np.float32)
    tab = jnp.einsum("hrj,jcw->hrcw", rpb.astype(F32), onehot, precision=HIGHEST)
    tab = jnp.where(col_ok[None, None], tab, NEG_INF)
    neg = jnp.full_like(tab, NEG_INF)
    both = jnp.concatenate([tab[:, :-1], tab[:, 1:]], axis=-1)
    left = jnp.concatenate([tab, neg], axis=-1)
    right = jnp.concatenate([neg, tab], axis=-1)
    none = jnp.concatenate([neg[:, :1], neg[:, :1]], axis=-1)
    return jnp.concatenate([both, left, right, none], axis=1)


def _na_attention(qkv, cache_k, cache_v, rpb):
    off = T_CTX // LAT_LEN
    qkv3 = qkv.reshape(T_ALL // LAT_LEN, LAT_LEN, 3 * D)
    ck = cache_k.reshape(N_LAT_SEQ, CTX_LEN, D)
    cv = cache_v.reshape(N_LAT_SEQ, CTX_LEN, D)
    slabs = _na_slab_table(rpb)
    nh = D // LANES
    out = pl.pallas_call(
        _na_kernel,
        out_shape=jax.ShapeDtypeStruct((N_LAT_SEQ, LAT_LEN, D), F32),
        grid=(N_LAT_SEQ, nh),
        in_specs=[pl.BlockSpec((1, LAT_LEN, LANES), lambda b, h: (b + off, 0, h)),
                  pl.BlockSpec((1, LAT_LEN, LANES), lambda b, h: (b + off, 0, nh + h)),
                  pl.BlockSpec((1, LAT_LEN, LANES), lambda b, h: (b + off, 0, 2 * nh + h)),
                  pl.BlockSpec((1, CTX_LEN, LANES), lambda b, h: (b, 0, h)),
                  pl.BlockSpec((1, CTX_LEN, LANES), lambda b, h: (b, 0, h)),
                  pl.BlockSpec((2, NA_NSLAB, GRID_W, 2 * GRID_W), lambda b, h: (h, 0, 0, 0))],
        out_specs=pl.BlockSpec((1, LAT_LEN, LANES), lambda b, h: (b, 0, h)),
        scratch_shapes=[pltpu.VMEM((2, 3, NA_QR * GRID_W, NA_KR * GRID_W), F32)],
        compiler_params=_cparams(2),
        name="neighbourhood_attention",
    )(qkv3, qkv3, qkv3, ck, cv, slabs)
    return out.reshape(T_LAT, D)


def _proj_res_kernel(ac_ref, al_ref, xc_ref, xl_ref, m_ref, g_ref, w_ref, o_ref, wb_ref, *, gate_row):
    i = pl.program_id(0)

    @pl.when(i == 0)
    def _():
        wb_ref[...] = w_ref[...].astype(BF16)

    a = jnp.where(i < CTX_TILES, ac_ref[...], al_ref[...])
    y = jnp.dot(a.astype(BF16), wb_ref[...], preferred_element_type=F32)
    x = jnp.where(i < CTX_TILES, xc_ref[...], xl_ref[...])
    o_ref[...] = x + m_ref[0, gate_row:gate_row + 1, :] * _rms(y, g_ref[...])


def _proj_residual(a_ctx, a_lat, x_ctx, x_lat, mod, g_post, w, gate_row):
    return pl.pallas_call(
        functools.partial(_proj_res_kernel, gate_row=gate_row),
        out_shape=jax.ShapeDtypeStruct((T_ALL, D), F32),
        grid=(T_ALL // TM,),
        in_specs=[pl.BlockSpec((TM, D), _ctx_part),
                  pl.BlockSpec((TM, D), _lat_part),
                  pl.BlockSpec((TM, D), _ctx_part),
                  pl.BlockSpec((TM, D), _lat_part),
                  pl.BlockSpec((1, 6, D), lambda i: (_mod_idx(i, TM), 0, 0)),
                  pl.BlockSpec((1, D), lambda i: (0, 0)),
                  pl.BlockSpec((D, D), lambda i: (0, 0))],
        out_specs=pl.BlockSpec((TM, D), lambda i: (i, 0)),
        scratch_shapes=[pltpu.VMEM((D, D), BF16)],
        compiler_params=_cparams(1),
        name="proj_residual",
    )(a_ctx, a_lat, x_ctx, x_lat, mod, g_post.reshape(1, D), w)


def _gla_gate_kernel(x_ref, g_ref, m_ref, w1_ref, w2_ref, b_ref, o_ref):
    h = _norm_mod(x_ref[...], g_ref[...], m_ref, 0)
    z = jnp.dot(h.astype(BF16), w1_ref[...].astype(BF16), preferred_element_type=F32)
    y = jnp.dot(z, w2_ref[...], precision=HIGHEST, preferred_element_type=F32) + b_ref[...]
    o_ref[...] = (jnp.minimum(y, 0.0) - jnp.log(1.0 + jnp.exp(-jnp.abs(y)))) * (1.0 / GLA_GATE_NORM)


def _gla_gates(x, g, mod, w1f, w2f, bf, w1b, w2b, bb):
    hk = GLA_HEADS * GLA_DK
    w1 = jnp.zeros((D, LANES), F32).at[:, :GLA_RANK].set(w1f).at[:, GLA_RANK:2 * GLA_RANK].set(w1b)
    w2 = jnp.zeros((LANES, 2 * hk), F32).at[:GLA_RANK, :hk].set(w2f).at[GLA_RANK:2 * GLA_RANK, hk:].set(w2b)
    b = jnp.concatenate([bf, bb]).reshape(1, 2 * hk)
    return pl.pallas_call(
        _gla_gate_kernel,
        out_shape=jax.ShapeDtypeStruct((T_ALL, 2 * hk), F32),
        grid=(T_ALL // TM,),
        in_specs=[pl.BlockSpec((TM, D), lambda i: (i, 0)),
                  pl.BlockSpec((1, D), lambda i: (0, 0)),
                  pl.BlockSpec((1, 6, D), lambda i: (_mod_idx(i, TM), 0, 0)),
                  pl.BlockSpec((D, LANES), lambda i: (0, 0)),
                  pl.BlockSpec((LANES, 2 * hk), lambda i: (0, 0)),
                  pl.BlockSpec((1, 2 * hk), lambda i: (0, 0))],
        out_specs=pl.BlockSpec((TM, 2 * hk), lambda i: (i, 0)),
        compiler_params=_cparams(1),
        name="gla_gates",
    )(x, g.reshape(1, D), mod, w1, w2, b)


def _rope(x, cos, sin_signed):
    lane = lax.broadcasted_iota(I32, (1, LANES), 1)
    partner = jnp.where((lane % 64) < 32, pltpu.roll(x, LANES - 32, 1), pltpu.roll(x, 32, 1))
    return x * cos + partner * sin_signed


def _gla_chunks(chunks, states):
    L = GLA_CHUNK
    cums = [jnp.dot(c["tri"].astype(F32), c["g"], precision=HIGHEST, preferred_element_type=F32) for c in chunks]
    cls = [cum[L - 1:L, :] if c["forward"] else cum[0:1, :] for c, cum in zip(chunks, cums)]
    q_decs = [(c["q"] * (GLA_DK ** -0.5) * jnp.exp(cum)).astype(BF16) for c, cum in zip(chunks, cums)]
    k_decs = [(c["k"] * jnp.exp(-cum)).astype(BF16) for c, cum in zip(chunks, cums)]
    k_rems = [(c["k"] * jnp.exp(cl - cum)).astype(BF16) for c, cum, cl in zip(chunks, cums, cls)]
    vbs = [c["v"].astype(BF16) for c in chunks]
    kv_ts = [lax.dot_general(vb, k_rem, _TN, preferred_element_type=F32) for vb, k_rem in zip(vbs, k_rems)]
    a_s = [lax.dot_general(q_dec, k_dec, _NT, preferred_element_type=F32) for q_dec, k_dec in zip(q_decs, k_decs)]
    a_s = [jnp.where(c["tri"], a, 0.0).astype(BF16) for c, a in zip(chunks, a_s)]
    outs = [jnp.dot(a, vb, preferred_element_type=F32) for a, vb in zip(a_s, vbs)]
    states = dict(states)
    for i, c in enumerate(chunks):
        st = states[c["scan"]]
        outs[i] = outs[i] + lax.dot_general(q_decs[i], st.astype(BF16), _NT, preferred_element_type=F32)
        states[c["scan"]] = jnp.exp(cls[i]) * st + kv_ts[i]
    return outs, states


def _gla_kernel(*refs, rotary, has_init, out_state, nblk):
    refs = list(refs)
    qf, kf, vf, gf, qb, kb, vb, gb = refs[:8]
    del refs[:8]
    if rotary:
        cosf, sinf, cosb, sinb = refs[:4]
        del refs[:4]
    if has_init:
        sf0, sb0 = refs[:2]
        del refs[:2]
    of, ob = refs[:2]
    del refs[:2]
    if out_state:
        sfo, sbo = refs[:2]
        del refs[:2]
    st_f, st_b = refs

    j = pl.program_id(2)

    @pl.when(j == 0)
    def _():
        if has_init:
            for hh in range(GLA_HG):
                st_f[hh] = sf0[0, hh].T
                st_b[hh] = sb0[0, hh].T
        else:
            st_f[...] = jnp.zeros_like(st_f)
            st_b[...] = jnp.zeros_like(st_b)

    L = GLA_CHUNK
    row = lax.broadcasted_iota(I32, (L, L), 0)
    col = lax.broadcasted_iota(I32, (L, L), 1)
    tri_f = col <= row
    tri_b = col >= row
    nchunk = GLA_BLOCK // L

    chunks, where, states = [], [], {}
    for hh in range(GLA_HG):
        ks = slice(hh * GLA_DK, (hh + 1) * GLA_DK)
        vs = slice(hh * GLA_DV, (hh + 1) * GLA_DV)
        states[(hh, True)] = st_f[hh]
        states[(hh, False)] = st_b[hh]
        for forward in (True, False):
            q_ref, k_ref, v_ref, g_ref = (qf, kf, vf, gf) if forward else (qb, kb, vb, gb)
            for c in (range(nchunk) if forward else reversed(range(nchunk))):
                sl = slice(c * L, (c + 1) * L)
                q, k = q_ref[0, sl, ks], k_ref[0, sl, ks]
                if rotary:
                    cos, sin = (cosf, sinf) if forward else (cosb, sinb)
                    q = _rope(q, cos[sl, :], sin[sl, :])
                    k = _rope(k, cos[sl, :], sin[sl, :])
                chunks.append(dict(q=q, k=k, v=v_ref[0, sl, vs], g=g_ref[0, sl, ks], forward=forward,
                                   tri=tri_f if forward else tri_b, scan=(hh, forward)))
                where.append((of if forward else ob, sl, vs))
    outs, states = _gla_chunks(chunks, states)
    for (o_ref, sl, vs), o in zip(where, outs):
        o_ref[0, sl, vs] = o
    for hh in range(GLA_HG):
        st_f[hh] = states[(hh, True)]
        st_b[hh] = states[(hh, False)]

    if out_state:
        @pl.when(j == nblk - 1)
        def _():
            for hh in range(GLA_HG):
                sfo[0, hh] = st_f[hh].T
                sbo[0, hh] = st_b[hh].T


def _rope_tables(n):
    t = np.arange(n)
    n_freq = GLA_DK // 4
    inv = ROPE_THETA ** (-np.arange(n_freq, dtype=np.float64) / n_freq)
    ang_r = (t // GRID_W).astype(np.float64)[:, None] * inv[None, :]
    ang_c = (t % GRID_W).astype(np.float64)[:, None] * inv[None, :]
    cos = np.concatenate([np.cos(ang_r), np.cos(ang_r), np.cos(ang_c), np.cos(ang_c)], axis=1)
    sin = np.concatenate([-np.sin(ang_r), np.sin(ang_r), -np.sin(ang_c), np.sin(ang_c)], axis=1)
    return jnp.asarray(cos, F32), jnp.asarray(sin, F32)


def _gla_scan(proj, gates, *, seq_len, n_seq, seq_off, s_f0=None, s_b0=None, rotary=False, out_state=False):
    nseq_all = T_ALL // seq_len
    nblk = seq_len // GLA_BLOCK
    nh = GLA_HEADS
    proj3 = proj.reshape(nseq_all, seq_len, 3 * D)
    g3 = gates.reshape(nseq_all, seq_len, 2 * nh * GLA_DK)
    has_init = s_f0 is not None

    def fwd(c0):
        return lambda b, h, j: (b + seq_off, j, c0 + h)

    def bwd(c0):
        return lambda b, h, j: (b + seq_off, nblk - 1 - j, c0 + h)

    ngrp = nh // GLA_HG
    qk = (1, GLA_BLOCK, GLA_HG * GLA_DK)
    vv = (1, GLA_BLOCK, GLA_HG * GLA_DV)
    in_specs = [pl.BlockSpec(qk, fwd(0)), pl.BlockSpec(qk, fwd(ngrp)), pl.BlockSpec(vv, fwd(ngrp)), pl.BlockSpec(qk, fwd(0)),
                pl.BlockSpec(qk, bwd(0)), pl.BlockSpec(qk, bwd(ngrp)), pl.BlockSpec(vv, bwd(ngrp)), pl.BlockSpec(qk, bwd(ngrp))]
    args = [proj3, proj3, proj3, g3, proj3, proj3, proj3, g3]
    if rotary:
        cos, sin = _rope_tables(seq_len)
        tab = (GLA_BLOCK, GLA_DK)
        in_specs += [pl.BlockSpec(tab, lambda b, h, j: (j, 0)), pl.BlockSpec(tab, lambda b, h, j: (j, 0)),
                     pl.BlockSpec(tab, lambda b, h, j: (nblk - 1 - j, 0)), pl.BlockSpec(tab, lambda b, h, j: (nblk - 1 - j, 0))]
        args += [cos, sin, cos, sin]
    st = (1, GLA_HG, GLA_DK, GLA_DV)
    if has_init:
        in_specs += [pl.BlockSpec(st, lambda b, h, j: (b, h, 0, 0))] * 2
        args += [s_f0, s_b0]
    out_shape = [jax.ShapeDtypeStruct((n_seq, seq_len, D), F32)] * 2
    out_specs = [pl.BlockSpec(vv, lambda b, h, j: (b, j, h)),
                 pl.BlockSpec(vv, lambda b, h, j: (b, nblk - 1 - j, h))]
    if out_state:
        out_shape += [jax.ShapeDtypeStruct((n_seq, nh, GLA_DK, GLA_DV), F32)] * 2
        out_specs += [pl.BlockSpec(st, lambda b, h, j: (b, h, 0, 0))] * 2
    res = pl.pallas_call(
        functools.partial(_gla_kernel, rotary=rotary, has_init=has_init, out_state=out_state, nblk=nblk),
        out_shape=out_shape,
        grid=(n_seq, ngrp, nblk),
        in_specs=in_specs,
        out_specs=out_specs,
        scratch_shapes=[pltpu.VMEM((GLA_HG, GLA_DV, GLA_DK), F32)] * 2,
        compiler_params=_cparams(3),
        name="gla_scan_rope" if rotary else "gla_scan",
    )(*args)
    of, ob = res[0].reshape(n_seq * seq_len, D), res[1].reshape(n_seq * seq_len, D)
    if out_state:
        return of, ob, res[2], res[3]
    return of, ob


def _gla_out_kernel(ofc_ref, obc_ref, ofl_ref, obl_ref, r_ref, ng_ref, x_ref, m_ref, g_ref, w_ref, o_ref, wb_ref):
    i = pl.program_id(0)

    @pl.when(i == 0)
    def _():
        wb_ref[...] = w_ref[...].astype(BF16)

    o = jnp.where(i < CTX_TILES, ofc_ref[...] + obc_ref[...], ofl_ref[...] + obl_ref[...])
    r = r_ref[...]
    ng = ng_ref[...]
    parts = []
    for h in range(GLA_HEADS):
        oh = o[:, h * GLA_DV:(h + 1) * GLA_DV]
        parts.append(_rms(oh, ng))
    a = jnp.concatenate(parts, axis=1) * (r * jax.nn.sigmoid(r))
    y = jnp.dot(a.astype(BF16), wb_ref[...], preferred_element_type=F32)
    o_ref[...] = x_ref[...] + m_ref[0, 2:3, :] * _rms(y, g_ref[...])


def _gla_output(of_ctx, ob_ctx, of_lat, ob_lat, proj, norm_g, x, mod, g_post, w):
    return pl.pallas_call(
        _gla_out_kernel,
        out_shape=jax.ShapeDtypeStruct((T_ALL, D), F32),
        grid=(T_ALL // TM,),
        in_specs=[pl.BlockSpec((TM, D), _ctx_part),
                  pl.BlockSpec((TM, D), _ctx_part),
                  pl.BlockSpec((TM, D), _lat_part),
                  pl.BlockSpec((TM, D), _lat_part),
                  pl.BlockSpec((TM, D), lambda i: (i, 2)),
                  pl.BlockSpec((1, GLA_DV), lambda i: (0, 0)),
                  pl.BlockSpec((TM, D), lambda i: (i, 0)),
                  pl.BlockSpec((1, 6, D), lambda i: (_mod_idx(i, TM), 0, 0)),
                  pl.BlockSpec((1, D), lambda i: (0, 0)),
                  pl.BlockSpec((D, D), lambda i: (0, 0))],
        out_specs=pl.BlockSpec((TM, D), lambda i: (i, 0)),
        scratch_shapes=[pltpu.VMEM((D, D), BF16)],
        compiler_params=_cparams(1),
        name="gla_output",
    )(of_ctx, ob_ctx, of_lat, ob_lat, proj, norm_g.reshape(1, GLA_DV), x, mod, g_post.reshape(1, D), w)


def _router_kernel(x_ref, g_ref, m_ref, wr_ref, br_ref, h_ref, ri_ref, rw_ref, cnt_ref, cnt_scr):
    @pl.when(pl.program_id(0) == 0)
    def _():
        cnt_scr[...] = jnp.zeros_like(cnt_scr)

    h = _norm_mod(x_ref[...], g_ref[...], m_ref, 3)
    h_ref[...] = h
    logits = jnp.dot(h, wr_ref[...], precision=HIGHEST, preferred_element_type=F32) + br_ref[...]
    lane = lax.broadcasted_iota(I32, (TM, LANES), 1)
    lane_f = lane.astype(F32)
    cur = jnp.where(lane < N_EXPERTS, logits, -jnp.inf)
    vals, sels = [], []
    hot = jnp.zeros((TM, LANES), F32)
    for _ in range(TOP_K):
        m = jnp.max(cur, axis=-1, keepdims=True)
        idx = jnp.min(jnp.where(cur == m, lane_f, float(LANES)), axis=-1, keepdims=True)
        sel = lane_f == idx
        vals.append(m)
        sels.append((idx, sel))
        hot = hot + sel.astype(F32)
        cur = jnp.where(sel, -jnp.inf, cur)
    ex = [jnp.exp(v - vals[0]) for v in vals]
    den = ex[0] + ex[1] + ex[2] + ex[3]
    r_i = lax.broadcasted_iota(I32, (TM, TM), 0)
    c_i = lax.broadcasted_iota(I32, (TM, TM), 1)
    before = (c_i < r_i).astype(BF16)
    prefix = jnp.dot(before, hot.astype(BF16), preferred_element_type=F32) + cnt_scr[0:1, :]
    ri = jnp.zeros((TM, LANES), F32)
    rw = jnp.zeros((TM, LANES), F32)
    for k in range(TOP_K):
        idx, sel = sels[k]
        rank = jnp.sum(jnp.where(sel, prefix, 0.0), axis=-1, keepdims=True)
        ri = jnp.where(lane == k, idx, ri)
        ri = jnp.where(lane == TOP_K + k, rank, ri)
        rw = jnp.where(lane == k, ex[k] / den, rw)
    ri_ref[...] = ri.T[0:2 * TOP_K, :].astype(I32)
    rw_ref[...] = rw
    cnt = cnt_scr[...] + jnp.sum(hot, axis=0, keepdims=True)
    cnt_scr[...] = cnt
    cnt_ref[...] = cnt


def _router(x, g, mod, w_router, b_router):
    wr = jnp.zeros((D, LANES), F32).at[:, :N_EXPERTS].set(w_router)
    br = jnp.zeros((1, LANES), F32).at[0, :N_EXPERTS].set(b_router)
    return pl.pallas_call(
        _router_kernel,
        out_shape=[jax.ShapeDtypeStruct((T_ALL, D), F32),
                   jax.ShapeDtypeStruct((2 * TOP_K, T_ALL), I32),
                   jax.ShapeDtypeStruct((T_ALL, LANES), F32),
                   jax.ShapeDtypeStruct((8, LANES), F32)],
        grid=(T_ALL // TM,),
        in_specs=[pl.BlockSpec((TM, D), lambda i: (i, 0)),
                  pl.BlockSpec((1, D), lambda i: (0, 0)),
                  pl.BlockSpec((1, 6, D), lambda i: (_mod_idx(i, TM), 0, 0)),
                  pl.BlockSpec((D, LANES), lambda i: (0, 0)),
                  pl.BlockSpec((1, LANES), lambda i: (0, 0))],
        out_specs=[pl.BlockSpec((TM, D), lambda i: (i, 0)),
                   pl.BlockSpec((2 * TOP_K, TM), lambda i: (0, i)),
                   pl.BlockSpec((TM, LANES), lambda i: (i, 0)),
                   pl.BlockSpec((8, LANES), lambda i: (0, 0))],
        scratch_shapes=[pltpu.VMEM((8, LANES), F32)],
        compiler_params=_cparams(1),
        name="moe_router",
    )(x, g.reshape(1, D), mod, wr, br)


def _dispatch_kernel(dest_ref, h_ref, xs_ref, sem):
    base = pl.program_id(0) * TD

    def row_copy(j, u, k):
        i0 = pl.multiple_of(j * DMA_GROUP, DMA_GROUP)
        d = dest_ref[k * T_ALL + base + i0 + u]
        return pltpu.make_async_copy(h_ref.at[pl.ds(i0, DMA_GROUP)].at[pl.ds(u, 1)], xs_ref.at[pl.ds(d, 1)], sem)

    def start(j, c):
        for u in range(DMA_GROUP):
            for k in range(TOP_K):
                row_copy(j, u, k).start(priority=k % 2)
        return c

    def wait(j, c):
        for u in range(DMA_GROUP):
            for k in range(TOP_K):
                row_copy(j, u, k).wait()
        return c

    lax.fori_loop(0, TD // DMA_GROUP, start, 0)
    lax.fori_loop(0, TD // DMA_GROUP, wait, 0)


def _dispatch(dest, h):
    return pl.pallas_call(
        _dispatch_kernel,
        out_shape=jax.ShapeDtypeStruct((MOE_ROWS, D), F32),
        grid_spec=pltpu.PrefetchScalarGridSpec(
            num_scalar_prefetch=1,
            grid=(T_ALL // TD,),
            in_specs=[pl.BlockSpec((TD, D), lambda i, dest: (i, 0))],
            out_specs=pl.BlockSpec(memory_space=pl.ANY),
            scratch_shapes=[pltpu.SemaphoreType.DMA(())]),
        compiler_params=_cparams(1),
        name="moe_dispatch",
    )(dest, h)


def _gmm_kernel(vt_ref, vg_ref, lo_ref, hi_ref, first_ref, x_ref, wu_ref, bu_ref, wd_ref, bd_ref, y_ref, wub, wdb):
    v = pl.program_id(0)
    g = vg_ref[v]
    new_group = jnp.logical_or(v == 0, vg_ref[jnp.maximum(v - 1, 0)] != g)

    @pl.when(new_group)
    def _():
        wub[...] = wu_ref[0, 0].astype(BF16)
        wdb[...] = wd_ref[0, 0].astype(BF16)

    lo = lo_ref[v]
    hi = hi_ref[v]

    def expert_rows():
        u = jnp.dot(x_ref[...].astype(BF16), wub[...], preferred_element_type=F32) + bu_ref[0, 0]
        gate = jnp.minimum(u[:, :D_FF], SWIGLU_LIMIT)
        lin = jnp.clip(u[:, D_FF:], -SWIGLU_LIMIT, SWIGLU_LIMIT)
        act = gate * jax.nn.sigmoid(SWIGLU_ALPHA * gate) * (lin + 1.0)
        y = jnp.dot(act.astype(BF16), wdb[...], preferred_element_type=F32) + bd_ref[0, 0]
        row = lax.broadcasted_iota(I32, (MOE_TM, 1), 0)
        return y, jnp.logical_and(row >= lo, row < hi)

    @pl.when(jnp.logical_and(hi > lo, first_ref[v] == 1))
    def _():
        y, mine = expert_rows()
        y_ref[...] = jnp.where(mine, y, 0.0)

    @pl.when(jnp.logical_and(hi > lo, first_ref[v] == 0))
    def _():
        y, mine = expert_rows()
        y_ref[...] = jnp.where(mine, y, y_ref[...])


def _moe_visits(counts):
    ends = jnp.cumsum(counts)
    starts = ends - counts
    first_tile = starts // MOE_TM
    last_tile = (ends - 1) // MOE_TM
    ntl = jnp.where(counts > 0, last_tile - first_tile + 1, 0)
    vend = jnp.cumsum(ntl)
    vstart = vend - ntl
    total = vend[-1]
    v = jnp.arange(MOE_VISITS, dtype=I32)
    vc = jnp.minimum(v, total - 1)
    grp = jnp.minimum(jnp.sum((vend[None, :] <= vc[:, None]).astype(I32), axis=1), N_EXPERTS - 1)
    tile = first_tile[grp] + (vc - vstart[grp])
    valid = v < total
    lo = jnp.where(valid, jnp.clip(starts[grp] - tile * MOE_TM, 0, MOE_TM), 0)
    hi = jnp.where(valid, jnp.clip(ends[grp] - tile * MOE_TM, 0, MOE_TM), 0)
    prev_tile = jnp.concatenate([jnp.full((1,), -1, I32), tile[:-1]])
    first = jnp.logical_and(valid, tile != prev_tile).astype(I32)
    return tile.astype(I32), grp, lo.astype(I32), hi.astype(I32), first, starts


def _grouped_mlp(visits, xs, layer, w_up, b_up, w_down, b_down):
    tile, grp, lo, hi, first = visits
    depth = w_up.shape[0]
    return pl.pallas_call(
        _gmm_kernel,
        out_shape=jax.ShapeDtypeStruct((MOE_ROWS, D), F32),
        grid_spec=pltpu.PrefetchScalarGridSpec(
            num_scalar_prefetch=5,
            grid=(MOE_VISITS,),
            in_specs=[pl.BlockSpec((MOE_TM, D), lambda v, vt, vg, lo, hi, fi: (vt[v], 0)),
                      pl.BlockSpec((1, 1, D, 2 * D_FF), lambda v, vt, vg, lo, hi, fi: (layer, vg[v], 0, 0)),
                      pl.BlockSpec((1, 1, 1, 2 * D_FF), lambda v, vt, vg, lo, hi, fi: (layer, vg[v], 0, 0)),
                      pl.BlockSpec((1, 1, D_FF, D), lambda v, vt, vg, lo, hi, fi: (layer, vg[v], 0, 0)),
                      pl.BlockSpec((1, 1, 1, D), lambda v, vt, vg, lo, hi, fi: (layer, vg[v], 0, 0))],
            out_specs=pl.BlockSpec((MOE_TM, D), lambda v, vt, vg, lo, hi, fi: (vt[v], 0)),
            scratch_shapes=[pltpu.VMEM((D, 2 * D_FF), BF16), pltpu.VMEM((D_FF, D), BF16)]),
        compiler_params=_cparams(1),
        name="moe_grouped_mlp",
    )(tile, grp, lo, hi, first, xs, w_up, b_up.reshape(depth, N_EXPERTS, 1, 2 * D_FF), w_down,
      b_down.reshape(depth, N_EXPERTS, 1, D))


def _combine_kernel(dest_ref, ys_ref, rw_ref, x_ref, m_ref, g_ref, *rest, split):
    if split:
        oc_ref, ol_ref, buf, sem = rest
    else:
        o_ref, buf, sem = rest
    step = pl.program_id(0)
    slot = step % 2

    def row_copy(s, j, u, k):
        i0 = pl.multiple_of(j * DMA_GROUP, DMA_GROUP)
        d = dest_ref[k * T_ALL + s * TC + i0 + u]
        return pltpu.make_async_copy(ys_ref.at[pl.ds(d, 1)], buf.at[s % 2, k, pl.ds(i0, DMA_GROUP)].at[pl.ds(u, 1)],
                                     sem.at[s % 2])

    def issue(s):
        def start(j, c):
            for u in range(DMA_GROUP):
                for k in range(TOP_K):
                    row_copy(s, j, u, k).start(priority=k % 2)
            return c

        lax.fori_loop(0, TC // DMA_GROUP, start, 0)

    @pl.when(step == 0)
    def _():
        issue(step)

    @pl.when(step + 1 < pl.num_programs(0))
    def _():
        issue(step + 1)

    def wait(j, c):
        for u in range(DMA_GROUP):
            for k in range(TOP_K):
                row_copy(step, j, u, k).wait()
        return c

    lax.fori_loop(0, TC // DMA_GROUP, wait, 0)
    w = rw_ref[...]
    f = w[:, 0:1] * buf[slot, 0]
    for k in range(1, TOP_K):
        f = f + w[:, k:k + 1] * buf[slot, k]
    res = x_ref[...] + m_ref[0, 5:6, :] * _rms(f, g_ref[...])
    if split:
        @pl.when(step < T_CTX // TC)
        def _():
            oc_ref[...] = res

        @pl.when(step >= T_CTX // TC)
        def _():
            ol_ref[...] = res
    else:
        o_ref[...] = res


def _combine(dest, ys, rw, x, mod, g_post, split):
    nc = T_CTX // TC
    if split:
        out_shape = [jax.ShapeDtypeStruct((T_CTX, D), F32), jax.ShapeDtypeStruct((T_LAT, D), F32)]
        out_specs = [pl.BlockSpec((TC, D), lambda i, dest: (jnp.minimum(i, nc - 1), 0)),
                     pl.BlockSpec((TC, D), lambda i, dest: (jnp.maximum(i - nc, 0), 0))]
    else:
        out_shape = jax.ShapeDtypeStruct((T_ALL, D), F32)
        out_specs = pl.BlockSpec((TC, D), lambda i, dest: (i, 0))
    return pl.pallas_call(
        functools.partial(_combine_kernel, split=split),
        out_shape=out_shape,
        grid_spec=pltpu.PrefetchScalarGridSpec(
            num_scalar_prefetch=1,
            grid=(T_ALL // TC,),
            in_specs=[pl.BlockSpec(memory_space=pl.ANY),
                      pl.BlockSpec((TC, LANES), lambda i, dest: (i, 0)),
                      pl.BlockSpec((TC, D), lambda i, dest: (i, 0)),
                      pl.BlockSpec((1, 6, D), lambda i, dest: (_mod_idx(i, TC), 0, 0)),
                      pl.BlockSpec((1, D), lambda i, dest: (0, 0))],
            out_specs=out_specs,
            scratch_shapes=[pltpu.VMEM((2, TOP_K, TC, D), F32), pltpu.SemaphoreType.DMA((2,))]),
        compiler_params=_cparams(1),
        name="moe_combine",
    )(dest, ys, rw, x, mod, g_post.reshape(1, D))


def _moe(x, mod, g_pre, g_post, w_router, b_router, layer, w_up, b_up, w_down, b_down, split_out=False):
    h, ri, rw, cnt = _router(x, g_pre, mod, w_router, b_router)
    counts = cnt[0, :N_EXPERTS].astype(I32)
    tile, grp, lo, hi, first, starts = _moe_visits(counts)
    experts = jnp.arange(N_EXPERTS, dtype=I32)[:, None, None]
    group_start = jnp.sum(jnp.where(ri[None, :TOP_K] == experts, starts.astype(I32)[:, None, None], 0), axis=0)
    dest = (group_start + ri[TOP_K:]).reshape(-1).astype(I32)
    xs = _dispatch(dest, h)
    ys = _grouped_mlp((tile, grp, lo, hi, first), xs, layer, w_up, b_up, w_down, b_down)
    return _combine(dest, ys, rw, x, mod, g_post, split_out)


def kernel(x_prompt, x_sample, cache_k, cache_v, state_fwd, state_bwd, c, c_ctx, w_ada, b_ada, g_pre_mix, g_post_mix, g_pre_ffn, g_post_ffn, na_w_qkv, na_rpb, na_w_out, gla_w_in, gla_w_g1_fwd, gla_w_g2_fwd, gla_b_g_fwd, gla_w_g1_bwd, gla_w_g2_bwd, gla_b_g_bwd, gla_norm_g, gla_w_out, moe_w_router, moe_b_router, moe_w_up, moe_b_up, moe_w_down, moe_b_down):
    x_ctx, x_lat = x_prompt.reshape(T_CTX, D), x_sample.reshape(T_LAT, D)
    cond = jnp.concatenate([c_ctx[None, :], c, jnp.zeros((8 - 1 - N_LAT_SEQ, D), F32)], axis=0)
    mods = _modulation(cond, w_ada, b_ada)

    qkv, new_k, new_v = _norm_mod_matmul((x_ctx, x_lat), g_pre_mix[0], mods[0], na_w_qkv[0], 0, kv_out=True)
    a_ctx = _ctx_attention(qkv)
    a_lat = _na_attention(qkv, cache_k[:, 0], cache_v[:, 0], na_rpb[0])
    x = _proj_residual(a_ctx, a_lat, x_ctx, x_lat, mods[0], g_post_mix[0], na_w_out[0], 2)
    new_k = new_k.reshape(N_CTX_SEQ, 1, CTX_LEN, NA_HEADS, NA_HD)
    new_v = new_v.reshape(N_CTX_SEQ, 1, CTX_LEN, NA_HEADS, NA_HD)
    x = _moe(x, mods[0], g_pre_ffn[0], g_post_ffn[0], moe_w_router[0], moe_b_router[0],
             0, moe_w_up, moe_b_up, moe_w_down, moe_b_down)

    proj = _norm_mod_matmul(x, g_pre_mix[1], mods[1], gla_w_in[0], 0)
    gates = _gla_gates(x, g_pre_mix[1], mods[1], gla_w_g1_fwd[0], gla_w_g2_fwd[0], gla_b_g_fwd[0],
                       gla_w_g1_bwd[0], gla_w_g2_bwd[0], gla_b_g_bwd[0])
    of_c, ob_c, s_f, s_b = _gla_scan(proj, gates, seq_len=CTX_LEN, n_seq=N_CTX_SEQ, seq_off=0, out_state=True)
    of_l, ob_l = _gla_scan(proj, gates, seq_len=LAT_LEN, n_seq=N_LAT_SEQ, seq_off=T_CTX // LAT_LEN,
                           s_f0=state_fwd[:, 0], s_b0=state_bwd[:, 0], rotary=True)
    x = _gla_output(of_c, ob_c, of_l, ob_l, proj, gla_norm_g[0], x, mods[1], g_post_mix[1], gla_w_out[0])
    x = _moe(x, mods[1], g_pre_ffn[1], g_post_ffn[1], moe_w_router[1], moe_b_router[1],
             1, moe_w_up, moe_b_up, moe_w_down, moe_b_down, split_out=True)

    return (x[0].reshape(N_CTX_SEQ, CTX_LEN, D), x[1].reshape(N_LAT_SEQ, LAT_LEN, D),
            new_k, new_v, s_f[:, None], s_b[:, None])
```

```python
import functools

import numpy as np
import jax
import jax.numpy as jnp
from jax import lax
from jax.experimental import pallas as pl
from jax.experimental.pallas import tpu as pltpu

F32 = jnp.float32
BF16 = jnp.bfloat16
I32 = jnp.int32
HIGHEST = lax.Precision.HIGHEST

D = 1024
N_CTX_SEQ = 32
CTX_LEN = 256
N_LAT_SEQ = 2
LAT_LEN = 4096
T_CTX = N_CTX_SEQ * CTX_LEN
T_LAT = N_LAT_SEQ * LAT_LEN
T_ALL = T_CTX + T_LAT
GRID_W = 64
GRID_ROWS = LAT_LEN // GRID_W
NA_HEADS = 16
NA_HD = 64
NA_WIN_ROWS = 8
NA_WIN_COLS = 16
GLA_HEADS = 4
GLA_DK = 128
GLA_DV = 256
GLA_RANK = 16
GLA_GATE_NORM = 16.0
GLA_CHUNK = 64
ROPE_THETA = 10000.0
N_EXPERTS = 32
TOP_K = 4
D_FF = 1024
SWIGLU_LIMIT = 7.0
SWIGLU_ALPHA = 1.702
EPS = 1e-6
NEG_INF = -1e30

LANES = 128
TM = 512
GLA_BLOCK = 256
GLA_HG = 4
MOE_TM = 512
MOE_ROWS = T_ALL * TOP_K
MOE_TILES = MOE_ROWS // MOE_TM
MOE_VISITS = MOE_TILES + N_EXPERTS - 1
TD = 512
TC = 256
DMA_GROUP = 8
VMEM_LIMIT = 60 * 1024 * 1024

_NT = (((1,), (1,)), ((), ()))
_TN = (((0,), (0,)), ((), ()))


def _cparams(n_axes, vmem=None):
    return pltpu.CompilerParams(
        dimension_semantics=("arbitrary",) * n_axes,
        vmem_limit_bytes=VMEM_LIMIT if vmem is None else vmem)


def _mod_idx(i, tm):
    return jnp.maximum((i * tm) // LAT_LEN - 1, 0)


def _rms(x, g):
    return x * lax.rsqrt(jnp.mean(x * x, axis=-1, keepdims=True) + EPS) * g


def _norm_mod(x, g, m_ref, shift_row):
    sh = m_ref[0, shift_row:shift_row + 1, :]
    sc = m_ref[0, shift_row + 1:shift_row + 2, :]
    return _rms(x, g) * (1.0 + sc) + sh


def _mod_kernel(c_ref, w_ref, b_ref, o_ref):
    c = c_ref[...]
    s = c * jax.nn.sigmoid(c)
    o_ref[0] = jnp.dot(s, w_ref[0], precision=HIGHEST, preferred_element_type=F32) + b_ref[0]


def _modulation(cond, w_ada, b_ada):
    depth = w_ada.shape[0]
    out = pl.pallas_call(
        _mod_kernel,
        out_shape=jax.ShapeDtypeStruct((depth, 8, 6 * D), F32),
        grid=(depth, 6),
        in_specs=[pl.BlockSpec((8, D), lambda l, j: (0, 0)),
                  pl.BlockSpec((1, D, D), lambda l, j: (l, 0, j)),
                  pl.BlockSpec((1, 1, D), lambda l, j: (l, 0, j))],
        out_specs=pl.BlockSpec((1, 8, D), lambda l, j: (l, 0, j)),
        compiler_params=_cparams(2),
        name="adaln_modulation",
    )(cond, w_ada, b_ada.reshape(depth, 1, 6 * D))
    return out.reshape(depth, 8, 6, D)


CTX_TILES = T_CTX // TM


def _ctx_part(i):
    return (jnp.minimum(i, CTX_TILES - 1), 0)


def _lat_part(i):
    return (jnp.maximum(i - CTX_TILES, 0), 0)


def _nmm_kernel(*refs, shift_row, split_x, kv_out, gates=False):
    refs = list(refs)
    i = pl.program_id(0)
    if split_x:
        xc_ref, xl_ref = refs[:2]
        del refs[:2]
        x = jnp.where(i < CTX_TILES, xc_ref[...], xl_ref[...])
    else:
        x = refs.pop(0)[...]
    g_ref, m_ref, w_ref = refs[:3]
    del refs[:3]
    if gates:
        w1_ref, w2_ref, bg_ref = refs[:3]
        del refs[:3]
    o_ref = refs.pop(0)
    if kv_out:
        k_ref, v_ref = refs[:2]
        del refs[:2]
    if gates:
        go_ref = refs.pop(0)
    (wb_ref,) = refs

    @pl.when(i == 0)
    def _():
        wb_ref[...] = w_ref[...].astype(BF16)

    h = _norm_mod(x, g_ref[...], m_ref, shift_row)
    res = jnp.dot(h.astype(BF16), wb_ref[...], preferred_element_type=F32)
    o_ref[...] = res
    if gates:
        z = jnp.dot(h.astype(BF16), w1_ref[...].astype(BF16), preferred_element_type=F32)
        y = jnp.dot(z, w2_ref[...], precision=HIGHEST, preferred_element_type=F32) + bg_ref[...]
        go_ref[...] = (jnp.minimum(y, 0.0) - jnp.log(1.0 + jnp.exp(-jnp.abs(y)))) * (1.0 / GLA_GATE_NORM)
    if kv_out:
        @pl.when(i < CTX_TILES)
        def _():
            k_ref[...] = res[:, D:2 * D]
            v_ref[...] = res[:, 2 * D:]


def _norm_mod_matmul(x, g, mod, w, shift_row, kv_out=False, gate_params=None):
    split_x = isinstance(x, tuple)
    n = w.shape[1]
    if split_x:
        x_specs = [pl.BlockSpec((TM, D), _ctx_part), pl.BlockSpec((TM, D), _lat_part)]
        x_args = list(x)
    else:
        x_specs = [pl.BlockSpec((TM, D), lambda i: (i, 0))]
        x_args = [x]
    out_shape = [jax.ShapeDtypeStruct((T_ALL, n), F32)]
    out_specs = [pl.BlockSpec((TM, n), lambda i: (i, 0))]
    if kv_out:
        out_shape += [jax.ShapeDtypeStruct((T_CTX, D), F32)] * 2
        out_specs += [pl.BlockSpec((TM, D), _ctx_part)] * 2
    gates = gate_params is not None
    g_specs, g_args = [], []
    if gates:
        g_args = list(gate_params)
        g_specs = [pl.BlockSpec(a.shape, lambda i: (0, 0)) for a in g_args]
        out_shape += [jax.ShapeDtypeStruct((T_ALL, g_args[1].shape[1]), F32)]
        out_specs += [pl.BlockSpec((TM, g_args[1].shape[1]), lambda i: (i, 0))]
    res = pl.pallas_call(
        functools.partial(_nmm_kernel, shift_row=shift_row, split_x=split_x, kv_out=kv_out, gates=gates),
        out_shape=out_shape,
        grid=(T_ALL // TM,),
        in_specs=x_specs + [pl.BlockSpec((1, D), lambda i: (0, 0)),
                            pl.BlockSpec((1, 6, D), lambda i: (_mod_idx(i, TM), 0, 0)),
                            pl.BlockSpec((D, n), lambda i: (0, 0), pipeline_mode=pl.Buffered(1))] + g_specs,
        out_specs=out_specs,
        scratch_shapes=[pltpu.VMEM((D, n), BF16)],
        compiler_params=_cparams(1),
        name="norm_mod_matmul",
    )(*x_args, g.reshape(1, D), mod, w, *g_args)
    return res if (kv_out or gates) else res[0]


def _ctx_attn_kernel(qkv_ref, o_ref):
    lane = lax.broadcasted_iota(I32, (1, LANES), 1)
    scale = NA_HD ** -0.5
    for hp in range(NA_HEADS // 2):
        q = qkv_ref[0, :, hp * LANES:(hp + 1) * LANES] * scale
        k = qkv_ref[0, :, D + hp * LANES:D + (hp + 1) * LANES].astype(BF16)
        v = qkv_ref[0, :, 2 * D + hp * LANES:2 * D + (hp + 1) * LANES]
        acc = jnp.zeros((CTX_LEN, LANES), F32)
        for half in range(2):
            msk = (lane < NA_HD) if half == 0 else (lane >= NA_HD)
            qm = jnp.where(msk, q, 0.0).astype(BF16)
            s = lax.dot_general(qm, k, _NT, preferred_element_type=F32)
            p = jnp.exp(s - jnp.max(s, axis=-1, keepdims=True))
            l = jnp.sum(p, axis=-1, keepdims=True)
            vm = jnp.where(msk, v, 0.0).astype(BF16)
            acc = acc + jnp.dot(p.astype(BF16), vm, preferred_element_type=F32) / l
        o_ref[0, :, hp * LANES:(hp + 1) * LANES] = acc


def _ctx_attention(qkv):
    out = pl.pallas_call(
        _ctx_attn_kernel,
        out_shape=jax.ShapeDtypeStruct((N_CTX_SEQ, CTX_LEN, D), F32),
        grid=(N_CTX_SEQ,),
        in_specs=[pl.BlockSpec((1, CTX_LEN, 3 * D), lambda b: (b, 0, 0))],
        out_specs=pl.BlockSpec((1, CTX_LEN, D), lambda b: (b, 0, 0)),
        compiler_params=_cparams(1),
        name="context_attention",
    )(qkv.reshape(T_ALL // CTX_LEN, CTX_LEN, 3 * D))
    return out.reshape(T_CTX, D)


NA_QR = 8
NA_KR = 2 * NA_QR
NA_NBLK = GRID_ROWS // NA_QR
NA_N_OFF = 2 * NA_WIN_ROWS - 1
NA_SLAB_BOTH, NA_SLAB_LEFT, NA_SLAB_RIGHT, NA_SLAB_NONE = 0, NA_N_OFF - 1, 2 * NA_N_OFF - 1, 3 * NA_N_OFF - 1
NA_NSLAB = 3 * NA_N_OFF


def _na_slab_ids():
    ids = np.zeros((3, NA_QR, NA_KR // 2), np.int64)
    half = NA_WIN_ROWS // 2
    starts = ((0, 0), (NA_QR, NA_QR - half), (GRID_ROWS - NA_QR, GRID_ROWS - NA_KR))
    for v, (rb, kb) in enumerate(starts):
        for qr in range(NA_QR):
            qa = rb + qr
            r0 = min(max(qa - half, 0), GRID_ROWS - NA_WIN_ROWS)
            for m in range(NA_KR // 2):
                kl, kr = kb + 2 * m, kb + 2 * m + 1
                vl, vr = r0 <= kl < r0 + NA_WIN_ROWS, r0 <= kr < r0 + NA_WIN_ROWS
                ol, orr = kl - qa + NA_WIN_ROWS - 1, kr - qa + NA_WIN_ROWS - 1
                if vl and vr:
                    ids[v, qr, m] = NA_SLAB_BOTH + ol
                elif vl:
                    ids[v, qr, m] = NA_SLAB_LEFT + ol
                elif vr:
                    ids[v, qr, m] = NA_SLAB_RIGHT + orr
                else:
                    ids[v, qr, m] = NA_SLAB_NONE
    return ids


def _na_kernel(q_ref, k_ref, v_ref, ck_ref, cv_ref, slab_ref, o_ref, bias_scr):
    slab_ids = _na_slab_ids()
    for half in range(2):
        for v in range(3):
            for qr in range(NA_QR):
                for m in range(NA_KR // 2):
                    bias_scr[half, v, qr * GRID_W:(qr + 1) * GRID_W, m * LANES:(m + 1) * LANES] = (
                        slab_ref[half, int(slab_ids[v, qr, m])])

    lane = lax.broadcasted_iota(I32, (1, LANES), 1)
    masks = ((lane < NA_HD), (lane >= NA_HD))
    scale = NA_HD ** -0.5
    ck = ck_ref[0].astype(BF16)
    cv = cv_ref[0]
    cvm = [jnp.where(m, cv, 0.0).astype(BF16) for m in masks]
    nq = NA_QR * GRID_W
    nk = NA_KR * GRID_W

    def body(blk, carry):
        rb = blk * NA_QR
        kb = jnp.clip(rb - NA_WIN_ROWS // 2, 0, GRID_ROWS - NA_KR)
        layout = jnp.where(blk == 0, 0, jnp.where(blk == NA_NBLK - 1, 2, 1))
        qs = pl.ds(pl.multiple_of(rb * GRID_W, nq), nq)
        ws = pl.ds(pl.multiple_of(kb * GRID_W, GRID_W * NA_WIN_ROWS // 2), nk)
        q = q_ref[0, qs, :] * scale
        kw = k_ref[0, ws, :].astype(BF16)
        vw = v_ref[0, ws, :]
        halves = range(2)
        qms = [jnp.where(masks[h], q, 0.0).astype(BF16) for h in halves]
        sws = [lax.dot_general(qms[h], kw, _NT, preferred_element_type=F32) + bias_scr[h, pl.ds(layout, 1)][0]
               for h in halves]
        scs = [lax.dot_general(qms[h], ck, _NT, preferred_element_type=F32) for h in halves]
        ms = [jnp.maximum(jnp.max(sws[h], axis=-1, keepdims=True), jnp.max(scs[h], axis=-1, keepdims=True))
              for h in halves]
        pws = [jnp.exp(sws[h] - ms[h]) for h in halves]
        pcs = [jnp.exp(scs[h] - ms[h]) for h in halves]
        ls = [jnp.sum(pws[h], axis=-1, keepdims=True) + jnp.sum(pcs[h], axis=-1, keepdims=True) for h in halves]
        vms = [jnp.where(masks[h], vw, 0.0).astype(BF16) for h in halves]
        os_ = [jnp.dot(pws[h].astype(BF16), vms[h], preferred_element_type=F32)
               + jnp.dot(pcs[h].astype(BF16), cvm[h], preferred_element_type=F32) for h in halves]
        o_ref[0, qs, :] = os_[0] / ls[0] + os_[1] / ls[1]
        return carry

    lax.fori_loop(0, NA_NBLK, body, 0)


def _na_slab_table(rpb):
    cidx = np.arange(GRID_W)
    col_start = np.clip(cidx - NA_WIN_COLS // 2, 0, GRID_W - NA_WIN_COLS)
    col_ok = (cidx[None, :] >= col_start[:, None]) & (cidx[None, :] < col_start[:, None] + NA_WIN_COLS)
    coff = np.clip(cidx[None, :] - cidx[:, None], -(NA_WIN_COLS - 1), NA_WIN_COLS - 1) + NA_WIN_COLS - 1
    n_coff = 2 * NA_WIN_COLS - 1
    onehot = (coff[None, :, :] == np.arange(n_coff)[:, None, None]).astype(np.float32)
    tab = jnp.einsum("hrj,jcw->hrcw", rpb.astype(F32), onehot, precision=HIGHEST)
    tab = jnp.where(col_ok[None, None], tab, NEG_INF)
    neg = jnp.full_like(tab, NEG_INF)
    both = jnp.concatenate([tab[:, :-1], tab[:, 1:]], axis=-1)
    left = jnp.concatenate([tab, neg], axis=-1)
    right = jnp.concatenate([neg, tab], axis=-1)
    none = jnp.concatenate([neg[:, :1], neg[:, :1]], axis=-1)
    return jnp.concatenate([both, left, right, none], axis=1)


def _na_attention(qkv, cache_k, cache_v, rpb):
    off = T_CTX // LAT_LEN
    qkv3 = qkv.reshape(T_ALL // LAT_LEN, LAT_LEN, 3 * D)
    ck = cache_k.reshape(N_LAT_SEQ, CTX_LEN, D)
    cv = cache_v.reshape(N_LAT_SEQ, CTX_LEN, D)
    slabs = _na_slab_table(rpb)
    nh = D // LANES
    out = pl.pallas_call(
        _na_kernel,
        out_shape=jax.ShapeDtypeStruct((N_LAT_SEQ, LAT_LEN, D), F32),
        grid=(N_LAT_SEQ, nh),
        in_specs=[pl.BlockSpec((1, LAT_LEN, LANES), lambda b, h: (b + off, 0, h)),
                  pl.BlockSpec((1, LAT_LEN, LANES), lambda b, h: (b + off, 0, nh + h)),
                  pl.BlockSpec((1, LAT_LEN, LANES), lambda b, h: (b + off, 0, 2 * nh + h)),
                  pl.BlockSpec((1, CTX_LEN, LANES), lambda b, h: (b, 0, h)),
                  pl.BlockSpec((1, CTX_LEN, LANES), lambda b, h: (b, 0, h)),
                  pl.BlockSpec((2, NA_NSLAB, GRID_W, 2 * GRID_W), lambda b, h: (h, 0, 0, 0))],
        out_specs=pl.BlockSpec((1, LAT_LEN, LANES), lambda b, h: (b, 0, h)),
        scratch_shapes=[pltpu.VMEM((2, 3, NA_QR * GRID_W, NA_KR * GRID_W), F32)],
        compiler_params=_cparams(2),
        name="neighbourhood_attention",
    )(qkv3, qkv3, qkv3, ck, cv, slabs)
    return out.reshape(T_LAT, D)


def _proj_res_kernel(ac_ref, al_ref, xc_ref, xl_ref, m_ref, g_ref, w_ref, o_ref, wb_ref, *, gate_row):
    i = pl.program_id(0)

    @pl.when(i == 0)
    def _():
        wb_ref[...] = w_ref[...].astype(BF16)

    a = jnp.where(i < CTX_TILES, ac_ref[...], al_ref[...])
    y = jnp.dot(a.astype(BF16), wb_ref[...], preferred_element_type=F32)
    x = jnp.where(i < CTX_TILES, xc_ref[...], xl_ref[...])
    o_ref[...] = x + m_ref[0, gate_row:gate_row + 1, :] * _rms(y, g_ref[...])


def _proj_residual(a_ctx, a_lat, x_ctx, x_lat, mod, g_post, w, gate_row):
    return pl.pallas_call(
        functools.partial(_proj_res_kernel, gate_row=gate_row),
        out_shape=jax.ShapeDtypeStruct((T_ALL, D), F32),
        grid=(T_ALL // TM,),
        in_specs=[pl.BlockSpec((TM, D), _ctx_part),
                  pl.BlockSpec((TM, D), _lat_part),
                  pl.BlockSpec((TM, D), _ctx_part),
                  pl.BlockSpec((TM, D), _lat_part),
                  pl.BlockSpec((1, 6, D), lambda i: (_mod_idx(i, TM), 0, 0)),
                  pl.BlockSpec((1, D), lambda i: (0, 0)),
                  pl.BlockSpec((D, D), lambda i: (0, 0))],
        out_specs=pl.BlockSpec((TM, D), lambda i: (i, 0)),
        scratch_shapes=[pltpu.VMEM((D, D), BF16)],
        compiler_params=_cparams(1),
        name="proj_residual",
    )(a_ctx, a_lat, x_ctx, x_lat, mod, g_post.reshape(1, D), w)


def _gla_gate_kernel(x_ref, g_ref, m_ref, w1_ref, w2_ref, b_ref, o_ref):
    h = _norm_mod(x_ref[...], g_ref[...], m_ref, 0)
    z = jnp.dot(h.astype(BF16), w1_ref[...].astype(BF16), preferred_element_type=F32)
    y = jnp.dot(z, w2_ref[...], precision=HIGHEST, preferred_element_type=F32) + b_ref[...]
    o_ref[...] = (jnp.minimum(y, 0.0) - jnp.log(1.0 + jnp.exp(-jnp.abs(y)))) * (1.0 / GLA_GATE_NORM)


def _gla_gates(x, g, mod, w1f, w2f, bf, w1b, w2b, bb):
    hk = GLA_HEADS * GLA_DK
    w1 = jnp.zeros((D, LANES), F32).at[:, :GLA_RANK].set(w1f).at[:, GLA_RANK:2 * GLA_RANK].set(w1b)
    w2 = jnp.zeros((LANES, 2 * hk), F32).at[:GLA_RANK, :hk].set(w2f).at[GLA_RANK:2 * GLA_RANK, hk:].set(w2b)
    b = jnp.concatenate([bf, bb]).reshape(1, 2 * hk)
    return pl.pallas_call(
        _gla_gate_kernel,
        out_shape=jax.ShapeDtypeStruct((T_ALL, 2 * hk), F32),
        grid=(T_ALL // TM,),
        in_specs=[pl.BlockSpec((TM, D), lambda i: (i, 0)),
                  pl.BlockSpec((1, D), lambda i: (0, 0)),
                  pl.BlockSpec((1, 6, D), lambda i: (_mod_idx(i, TM), 0, 0)),
                  pl.BlockSpec((D, LANES), lambda i: (0, 0)),
                  pl.BlockSpec((LANES, 2 * hk), lambda i: (0, 0)),
                  pl.BlockSpec((1, 2 * hk), lambda i: (0, 0))],
        out_specs=pl.BlockSpec((TM, 2 * hk), lambda i: (i, 0)),
        compiler_params=_cparams(1),
        name="gla_gates",
    )(x, g.reshape(1, D), mod, w1, w2, b)


def _rope(x, cos, sin_signed):
    lane = lax.broadcasted_iota(I32, (1, LANES), 1)
    partner = jnp.where((lane % 64) < 32, pltpu.roll(x, LANES - 32, 1), pltpu.roll(x, 32, 1))
    return x * cos + partner * sin_signed


def _gla_chunks(chunks, states):
    L = GLA_CHUNK
    cums = [jnp.dot(c["tri"].astype(F32), c["g"], precision=HIGHEST, preferred_element_type=F32) for c in chunks]
    cls = [cum[L - 1:L, :] if c["forward"] else cum[0:1, :] for c, cum in zip(chunks, cums)]
    q_decs = [(c["q"] * (GLA_DK ** -0.5) * jnp.exp(cum)).astype(BF16) for c, cum in zip(chunks, cums)]
    k_decs = [(c["k"] * jnp.exp(-cum)).astype(BF16) for c, cum in zip(chunks, cums)]
    k_rems = [(c["k"] * jnp.exp(cl - cum)).astype(BF16) for c, cum, cl in zip(chunks, cums, cls)]
    vbs = [c["v"].astype(BF16) for c in chunks]
    kv_ts = [lax.dot_general(vb, k_rem, _TN, preferred_element_type=F32) for vb, k_rem in zip(vbs, k_rems)]
    a_s = [lax.dot_general(q_dec, k_dec, _NT, preferred_element_type=F32) for q_dec, k_dec in zip(q_decs, k_decs)]
    a_s = [jnp.where(c["tri"], a, 0.0).astype(BF16) for c, a in zip(chunks, a_s)]
    outs = [jnp.dot(a, vb, preferred_element_type=F32) for a, vb in zip(a_s, vbs)]
    states = dict(states)
    for i, c in enumerate(chunks):
        st = states[c["scan"]]
        outs[i] = outs[i] + lax.dot_general(q_decs[i], st.astype(BF16), _NT, preferred_element_type=F32)
        states[c["scan"]] = jnp.exp(cls[i]) * st + kv_ts[i]
    return outs, states


def _gla_kernel(*refs, rotary, has_init, out_state, nblk):
    refs = list(refs)
    qf, kf, vf, gf, qb, kb, vb, gb = refs[:8]
    del refs[:8]
    if rotary:
        cosf, sinf, cosb, sinb = refs[:4]
        del refs[:4]
    if has_init:
        sf0, sb0 = refs[:2]
        del refs[:2]
    of, ob = refs[:2]
    del refs[:2]
    if out_state:
        sfo, sbo = refs[:2]
        del refs[:2]
    st_f, st_b = refs

    j = pl.program_id(2)

    @pl.when(j == 0)
    def _():
        if has_init:
            for hh in range(GLA_HG):
                st_f[hh] = sf0[0, hh].T
                st_b[hh] = sb0[0, hh].T
        else:
            st_f[...] = jnp.zeros_like(st_f)
            st_b[...] = jnp.zeros_like(st_b)

    L = GLA_CHUNK
    row = lax.broadcasted_iota(I32, (L, L), 0)
    col = lax.broadcasted_iota(I32, (L, L), 1)
    tri_f = col <= row
    tri_b = col >= row
    nchunk = GLA_BLOCK // L

    chunks, where, states = [], [], {}
    for hh in range(GLA_HG):
        ks = slice(hh * GLA_DK, (hh + 1) * GLA_DK)
        vs = slice(hh * GLA_DV, (hh + 1) * GLA_DV)
        states[(hh, True)] = st_f[hh]
        states[(hh, False)] = st_b[hh]
        for forward in (True, False):
            q_ref, k_ref, v_ref, g_ref = (qf, kf, vf, gf) if forward else (qb, kb, vb, gb)
            for c in (range(nchunk) if forward else reversed(range(nchunk))):
                sl = slice(c * L, (c + 1) * L)
                q, k = q_ref[0, sl, ks], k_ref[0, sl, ks]
                if rotary:
                    cos, sin = (cosf, sinf) if forward else (cosb, sinb)
                    q = _rope(q, cos[sl, :], sin[sl, :])
                    k = _rope(k, cos[sl, :], sin[sl, :])
                chunks.append(dict(q=q, k=k, v=v_ref[0, sl, vs], g=g_ref[0, sl, ks], forward=forward,
                                   tri=tri_f if forward else tri_b, scan=(hh, forward)))
                where.append((of if forward else ob, sl, vs))
    outs, states = _gla_chunks(chunks, states)
    for (o_ref, sl, vs), o in zip(where, outs):
        o_ref[0, sl, vs] = o
    for hh in range(GLA_HG):
        st_f[hh] = states[(hh, True)]
        st_b[hh] = states[(hh, False)]

    if out_state:
        @pl.when(j == nblk - 1)
        def _():
            for hh in range(GLA_HG):
                sfo[0, hh] = st_f[hh].T
                sbo[0, hh] = st_b[hh].T


def _rope_tables(n):
    t = np.arange(n)
    n_freq = GLA_DK // 4
    inv = ROPE_THETA ** (-np.arange(n_freq, dtype=np.float64) / n_freq)
    ang_r = (t // GRID_W).astype(np.float64)[:, None] * inv[None, :]
    ang_c = (t % GRID_W).astype(np.float64)[:, None] * inv[None, :]
    cos = np.concatenate([np.cos(ang_r), np.cos(ang_r), np.cos(ang_c), np.cos(ang_c)], axis=1)
    sin = np.concatenate([-np.sin(ang_r), np.sin(ang_r), -np.sin(ang_c), np.sin(ang_c)], axis=1)
    return jnp.asarray(cos, F32), jnp.asarray(sin, F32)


def _gla_scan(proj, gates, *, seq_len, n_seq, seq_off, s_f0=None, s_b0=None, rotary=False, out_state=False):
    nseq_all = T_ALL // seq_len
    nblk = seq_len // GLA_BLOCK
    nh = GLA_HEADS
    proj3 = proj.reshape(nseq_all, seq_len, 3 * D)
    g3 = gates.reshape(nseq_all, seq_len, 2 * nh * GLA_DK)
    has_init = s_f0 is not None

    def fwd(c0):
        return lambda b, h, j: (b + seq_off, j, c0 + h)

    def bwd(c0):
        return lambda b, h, j: (b + seq_off, nblk - 1 - j, c0 + h)

    ngrp = nh // GLA_HG
    qk = (1, GLA_BLOCK, GLA_HG * GLA_DK)
    vv = (1, GLA_BLOCK, GLA_HG * GLA_DV)
    in_specs = [pl.BlockSpec(qk, fwd(0)), pl.BlockSpec(qk, fwd(ngrp)), pl.BlockSpec(vv, fwd(ngrp)), pl.BlockSpec(qk, fwd(0)),
                pl.BlockSpec(qk, bwd(0)), pl.BlockSpec(qk, bwd(ngrp)), pl.BlockSpec(vv, bwd(ngrp)), pl.BlockSpec(qk, bwd(ngrp))]
    args = [proj3, proj3, proj3, g3, proj3, proj3, proj3, g3]
    if rotary:
        cos, sin = _rope_tables(seq_len)
        tab = (GLA_BLOCK, GLA_DK)
        in_specs += [pl.BlockSpec(tab, lambda b, h, j: (j, 0)), pl.BlockSpec(tab, lambda b, h, j: (j, 0)),
                     pl.BlockSpec(tab, lambda b, h, j: (nblk - 1 - j, 0)), pl.BlockSpec(tab, lambda b, h, j: (nblk - 1 - j, 0))]
        args += [cos, sin, cos, sin]
    st = (1, GLA_HG, GLA_DK, GLA_DV)
    if has_init:
        in_specs += [pl.BlockSpec(st, lambda b, h, j: (b, h, 0, 0))] * 2
        args += [s_f0, s_b0]
    out_shape = [jax.ShapeDtypeStruct((n_seq, seq_len, D), F32)] * 2
    out_specs = [pl.BlockSpec(vv, lambda b, h, j: (b, j, h)),
                 pl.BlockSpec(vv, lambda b, h, j: (b, nblk - 1 - j, h))]
    if out_state:
        out_shape += [jax.ShapeDtypeStruct((n_seq, nh, GLA_DK, GLA_DV), F32)] * 2
        out_specs += [pl.BlockSpec(st, lambda b, h, j: (b, h, 0, 0))] * 2
    res = pl.pallas_call(
        functools.partial(_gla_kernel, rotary=rotary, has_init=has_init, out_state=out_state, nblk=nblk),
        out_shape=out_shape,
        grid=(n_seq, ngrp, nblk),
        in_specs=in_specs,
        out_specs=out_specs,
        scratch_shapes=[pltpu.VMEM((GLA_HG, GLA_DV, GLA_DK), F32)] * 2,
        compiler_params=_cparams(3),
        name="gla_scan_rope" if rotary else "gla_scan",
    )(*args)
    of, ob = res[0].reshape(n_seq * seq_len, D), res[1].reshape(n_seq * seq_len, D)
    if out_state:
        return of, ob, res[2], res[3]
    return of, ob


def _gla_out_kernel(ofc_ref, obc_ref, ofl_ref, obl_ref, r_ref, ng_ref, x_ref, m_ref, g_ref, w_ref, o_ref, wb_ref):
    i = pl.program_id(0)

    @pl.when(i == 0)
    def _():
        wb_ref[...] = w_ref[...].astype(BF16)

    o = jnp.where(i < CTX_TILES, ofc_ref[...] + obc_ref[...], ofl_ref[...] + obl_ref[...])
    r = r_ref[...]
    ng = ng_ref[...]
    parts = []
    for h in range(GLA_HEADS):
        oh = o[:, h * GLA_DV:(h + 1) * GLA_DV]
        parts.append(_rms(oh, ng))
    a = jnp.concatenate(parts, axis=1) * (r * jax.nn.sigmoid(r))
    y = jnp.dot(a.astype(BF16), wb_ref[...], preferred_element_type=F32)
    o_ref[...] = x_ref[...] + m_ref[0, 2:3, :] * _rms(y, g_ref[...])


def _gla_output(of_ctx, ob_ctx, of_lat, ob_lat, proj, norm_g, x, mod, g_post, w):
    return pl.pallas_call(
        _gla_out_kernel,
        out_shape=jax.ShapeDtypeStruct((T_ALL, D), F32),
        grid=(T_ALL // TM,),
        in_specs=[pl.BlockSpec((TM, D), _ctx_part),
                  pl.BlockSpec((TM, D), _ctx_part),
                  pl.BlockSpec((TM, D), _lat_part),
                  pl.BlockSpec((TM, D), _lat_part),
                  pl.BlockSpec((TM, D), lambda i: (i, 2)),
                  pl.BlockSpec((1, GLA_DV), lambda i: (0, 0)),
                  pl.BlockSpec((TM, D), lambda i: (i, 0)),
                  pl.BlockSpec((1, 6, D), lambda i: (_mod_idx(i, TM), 0, 0)),
                  pl.BlockSpec((1, D), lambda i: (0, 0)),
                  pl.BlockSpec((D, D), lambda i: (0, 0))],
        out_specs=pl.BlockSpec((TM, D), lambda i: (i, 0)),
        scratch_shapes=[pltpu.VMEM((D, D), BF16)],
        compiler_params=_cparams(1),
        name="gla_output",
    )(of_ctx, ob_ctx, of_lat, ob_lat, proj, norm_g.reshape(1, GLA_DV), x, mod, g_post.reshape(1, D), w)


def _router_kernel(x_ref, g_ref, m_ref, wr_ref, br_ref, h_ref, ri_ref, rw_ref, cnt_ref, cnt_scr):
    @pl.when(pl.program_id(0) == 0)
    def _():
        cnt_scr[...] = jnp.zeros_like(cnt_scr)

    h = _norm_mod(x_ref[...], g_ref[...], m_ref, 3)
    h_ref[...] = h
    logits = jnp.dot(h, wr_ref[...], precision=HIGHEST, preferred_element_type=F32) + br_ref[...]
    lane = lax.broadcasted_iota(I32, (TM, LANES), 1)
    lane_f = lane.astype(F32)
    cur = jnp.where(lane < N_EXPERTS, logits, -jnp.inf)
    vals, sels = [], []
    hot = jnp.zeros((TM, LANES), F32)
    for _ in range(TOP_K):
        m = jnp.max(cur, axis=-1, keepdims=True)
        idx = jnp.min(jnp.where(cur == m, lane_f, float(LANES)), axis=-1, keepdims=True)
        sel = lane_f == idx
        vals.append(m)
        sels.append((idx, sel))
        hot = hot + sel.astype(F32)
        cur = jnp.where(sel, -jnp.inf, cur)
    ex = [jnp.exp(v - vals[0]) for v in vals]
    den = ex[0] + ex[1] + ex[2] + ex[3]
    r_i = lax.broadcasted_iota(I32, (TM, TM), 0)
    c_i = lax.broadcasted_iota(I32, (TM, TM), 1)
    before = (c_i < r_i).astype(BF16)
    prefix = jnp.dot(before, hot.astype(BF16), preferred_element_type=F32) + cnt_scr[0:1, :]
    ri = jnp.zeros((TM, LANES), F32)
    rw = jnp.zeros((TM, LANES), F32)
    for k in range(TOP_K):
        idx, sel = sels[k]
        rank = jnp.sum(jnp.where(sel, prefix, 0.0), axis=-1, keepdims=True)
        ri = jnp.where(lane == k, idx, ri)
        ri = jnp.where(lane == TOP_K + k, rank, ri)
        rw = jnp.where(lane == k, ex[k] / den, rw)
    ri_ref[...] = ri.T[0:2 * TOP_K, :].astype(I32)
    rw_ref[...] = rw
    cnt = cnt_scr[...] + jnp.sum(hot, axis=0, keepdims=True)
    cnt_scr[...] = cnt
    cnt_ref[...] = cnt


def _router(x, g, mod, w_router, b_router):
    wr = jnp.zeros((D, LANES), F32).at[:, :N_EXPERTS].set(w_router)
    br = jnp.zeros((1, LANES), F32).at[0, :N_EXPERTS].set(b_router)
    return pl.pallas_call(
        _router_kernel,
        out_shape=[jax.ShapeDtypeStruct((T_ALL, D), F32),
                   jax.ShapeDtypeStruct((2 * TOP_K, T_ALL), I32),
                   jax.ShapeDtypeStruct((T_ALL, LANES), F32),
                   jax.ShapeDtypeStruct((8, LANES), F32)],
        grid=(T_ALL // TM,),
        in_specs=[pl.BlockSpec((TM, D), lambda i: (i, 0)),
                  pl.BlockSpec((1, D), lambda i: (0, 0)),
                  pl.BlockSpec((1, 6, D), lambda i: (_mod_idx(i, TM), 0, 0)),
                  pl.BlockSpec((D, LANES), lambda i: (0, 0)),
                  pl.BlockSpec((1, LANES), lambda i: (0, 0))],
        out_specs=[pl.BlockSpec((TM, D), lambda i: (i, 0)),
                   pl.BlockSpec((2 * TOP_K, TM), lambda i: (0, i)),
                   pl.BlockSpec((TM, LANES), lambda i: (i, 0)),
                   pl.BlockSpec((8, LANES), lambda i: (0, 0))],
        scratch_shapes=[pltpu.VMEM((8, LANES), F32)],
        compiler_params=_cparams(1),
        name="moe_router",
    )(x, g.reshape(1, D), mod, wr, br)


def _dispatch_kernel(dest_ref, h_ref, xs_ref, sem):
    base = pl.program_id(0) * TD

    def row_copy(j, u, k):
        i0 = pl.multiple_of(j * DMA_GROUP, DMA_GROUP)
        d = dest_ref[k * T_ALL + base + i0 + u]
        return pltpu.make_async_copy(h_ref.at[pl.ds(i0, DMA_GROUP)].at[pl.ds(u, 1)], xs_ref.at[pl.ds(d, 1)], sem)

    def start(j, c):
        for u in range(DMA_GROUP):
            for k in range(TOP_K):
                row_copy(j, u, k).start(priority=k % 2)
        return c

    def wait(j, c):
        for u in range(DMA_GROUP):
            for k in range(TOP_K):
                row_copy(j, u, k).wait()
        return c

    lax.fori_loop(0, TD // DMA_GROUP, start, 0)
    lax.fori_loop(0, TD // DMA_GROUP, wait, 0)


def _dispatch(dest, h):
    return pl.pallas_call(
        _dispatch_kernel,
        out_shape=jax.ShapeDtypeStruct((MOE_ROWS, D), F32),
        grid_spec=pltpu.PrefetchScalarGridSpec(
            num_scalar_prefetch=1,
            grid=(T_ALL // TD,),
            in_specs=[pl.BlockSpec((TD, D), lambda i, dest: (i, 0))],
            out_specs=pl.BlockSpec(memory_space=pl.ANY),
            scratch_shapes=[pltpu.SemaphoreType.DMA(())]),
        compiler_params=_cparams(1),
        name="moe_dispatch",
    )(dest, h)


def _gmm_kernel(vt_ref, vg_ref, lo_ref, hi_ref, first_ref, x_ref, wu_ref, bu_ref, wd_ref, bd_ref, y_ref, wub, wdb):
    v = pl.program_id(0)
    g = vg_ref[v]
    new_group = jnp.logical_or(v == 0, vg_ref[jnp.maximum(v - 1, 0)] != g)

    @pl.when(new_group)
    def _():
        wub[...] = wu_ref[0, 0].astype(BF16)
        wdb[...] = wd_ref[0, 0].astype(BF16)

    lo = lo_ref[v]
    hi = hi_ref[v]

    def expert_rows():
        u = jnp.dot(x_ref[...].astype(BF16), wub[...], preferred_element_type=F32) + bu_ref[0, 0]
        gate = jnp.minimum(u[:, :D_FF], SWIGLU_LIMIT)
        lin = jnp.clip(u[:, D_FF:], -SWIGLU_LIMIT, SWIGLU_LIMIT)
        act = gate * jax.nn.sigmoid(SWIGLU_ALPHA * gate) * (lin + 1.0)
        y = jnp.dot(act.astype(BF16), wdb[...], preferred_element_type=F32) + bd_ref[0, 0]
        row = lax.broadcasted_iota(I32, (MOE_TM, 1), 0)
        return y, jnp.logical_and(row >= lo, row < hi)

    @pl.when(jnp.logical_and(hi > lo, first_ref[v] == 1))
    def _():
        y, mine = expert_rows()
        y_ref[...] = jnp.where(mine, y, 0.0)

    @pl.when(jnp.logical_and(hi > lo, first_ref[v] == 0))
    def _():
        y, mine = expert_rows()
        y_ref[...] = jnp.where(mine, y, y_ref[...])


def _moe_visits(counts):
    ends = jnp.cumsum(counts)
    starts = ends - counts
    first_tile = starts // MOE_TM
    last_tile = (ends - 1) // MOE_TM
    ntl = jnp.where(counts > 0, last_tile - first_tile + 1, 0)
    vend = jnp.cumsum(ntl)
    vstart = vend - ntl
    total = vend[-1]
    v = jnp.arange(MOE_VISITS, dtype=I32)
    vc = jnp.minimum(v, total - 1)
    grp = jnp.minimum(jnp.sum((vend[None, :] <= vc[:, None]).astype(I32), axis=1), N_EXPERTS - 1)
    tile = first_tile[grp] + (vc - vstart[grp])
    valid = v < total
    lo = jnp.where(valid, jnp.clip(starts[grp] - tile * MOE_TM, 0, MOE_TM), 0)
    hi = jnp.where(valid, jnp.clip(ends[grp] - tile * MOE_TM, 0, MOE_TM), 0)
    prev_tile = jnp.concatenate([jnp.full((1,), -1, I32), tile[:-1]])
    first = jnp.logical_and(valid, tile != prev_tile).astype(I32)
    return tile.astype(I32), grp, lo.astype(I32), hi.astype(I32), first, starts


def _grouped_mlp(visits, xs, layer, w_up, b_up, w_down, b_down):
    tile, grp, lo, hi, first = visits
    depth = w_up.shape[0]
    return pl.pallas_call(
        _gmm_kernel,
        out_shape=jax.ShapeDtypeStruct((MOE_ROWS, D), F32),
        grid_spec=pltpu.PrefetchScalarGridSpec(
            num_scalar_prefetch=5,
            grid=(MOE_VISITS,),
            in_specs=[pl.BlockSpec((MOE_TM, D), lambda v, vt, vg, lo, hi, fi: (vt[v], 0)),
                      pl.BlockSpec((1, 1, D, 2 * D_FF), lambda v, vt, vg, lo, hi, fi: (layer, vg[v], 0, 0)),
                      pl.BlockSpec((1, 1, 1, 2 * D_FF), lambda v, vt, vg, lo, hi, fi: (layer, vg[v], 0, 0)),
                      pl.BlockSpec((1, 1, D_FF, D), lambda v, vt, vg, lo, hi, fi: (layer, vg[v], 0, 0)),
                      pl.BlockSpec((1, 1, 1, D), lambda v, vt, vg, lo, hi, fi: (layer, vg[v], 0, 0))],
            out_specs=pl.BlockSpec((MOE_TM, D), lambda v, vt, vg, lo, hi, fi: (vt[v], 0)),
            scratch_shapes=[pltpu.VMEM((D, 2 * D_FF), BF16), pltpu.VMEM((D_FF, D), BF16)]),
        compiler_params=_cparams(1),
        name="moe_grouped_mlp",
    )(tile, grp, lo, hi, first, xs, w_up, b_up.reshape(depth, N_EXPERTS, 1, 2 * D_FF), w_down,
      b_down.reshape(depth, N_EXPERTS, 1, D))


def _combine_kernel(dest_ref, ys_ref, rw_ref, x_ref, m_ref, g_ref, *rest, split):
    if split:
        oc_ref, ol_ref, buf, sem = rest
    else:
        o_ref, buf, sem = rest
    step = pl.program_id(0)
    slot = step % 2

    def row_copy(s, j, u, k):
        i0 = pl.multiple_of(j * DMA_GROUP, DMA_GROUP)
        d = dest_ref[k * T_ALL + s * TC + i0 + u]
        return pltpu.make_async_copy(ys_ref.at[pl.ds(d, 1)], buf.at[s % 2, k, pl.ds(i0, DMA_GROUP)].at[pl.ds(u, 1)],
                                     sem.at[s % 2])

    def issue(s):
        def start(j, c):
            for u in range(DMA_GROUP):
                for k in range(TOP_K):
                    row_copy(s, j, u, k).start(priority=k % 2)
            return c

        lax.fori_loop(0, TC // DMA_GROUP, start, 0)

    @pl.when(step == 0)
    def _():
        issue(step)

    @pl.when(step + 1 < pl.num_programs(0))
    def _():
        issue(step + 1)

    def wait(j, c):
        for u in range(DMA_GROUP):
            for k in range(TOP_K):
                row_copy(step, j, u, k).wait()
        return c

    lax.fori_loop(0, TC // DMA_GROUP, wait, 0)
    w = rw_ref[...]
    f = w[:, 0:1] * buf[slot, 0]
    for k in range(1, TOP_K):
        f = f + w[:, k:k + 1] * buf[slot, k]
    res = x_ref[...] + m_ref[0, 5:6, :] * _rms(f, g_ref[...])
    if split:
        @pl.when(step < T_CTX // TC)
        def _():
            oc_ref[...] = res

        @pl.when(step >= T_CTX // TC)
        def _():
            ol_ref[...] = res
    else:
        o_ref[...] = res


def _combine(dest, ys, rw, x, mod, g_post, split):
    nc = T_CTX // TC
    if split:
        out_shape = [jax.ShapeDtypeStruct((T_CTX, D), F32), jax.ShapeDtypeStruct((T_LAT, D), F32)]
        out_specs = [pl.BlockSpec((TC, D), lambda i, dest: (jnp.minimum(i, nc - 1), 0)),
                     pl.BlockSpec((TC, D), lambda i, dest: (jnp.maximum(i - nc, 0), 0))]
    else:
        out_shape = jax.ShapeDtypeStruct((T_ALL, D), F32)
        out_specs = pl.BlockSpec((TC, D), lambda i, dest: (i, 0))
    return pl.pallas_call(
        functools.partial(_combine_kernel, split=split),
        out_shape=out_shape,
        grid_spec=pltpu.PrefetchScalarGridSpec(
            num_scalar_prefetch=1,
            grid=(T_ALL // TC,),
            in_specs=[pl.BlockSpec(memory_space=pl.ANY),
                      pl.BlockSpec((TC, LANES), lambda i, dest: (i, 0)),
                      pl.BlockSpec((TC, D), lambda i, dest: (i, 0)),
                      pl.BlockSpec((1, 6, D), lambda i, dest: (_mod_idx(i, TC), 0, 0)),
                      pl.BlockSpec((1, D), lambda i, dest: (0, 0))],
            out_specs=out_specs,
            scratch_shapes=[pltpu.VMEM((2, TOP_K, TC, D), F32), pltpu.SemaphoreType.DMA((2,))]),
        compiler_params=_cparams(1),
        name="moe_combine",
    )(dest, ys, rw, x, mod, g_post.reshape(1, D))


def _moe(x, mod, g_pre, g_post, w_router, b_router, layer, w_up, b_up, w_down, b_down, split_out=False):
    h, ri, rw, cnt = _router(x, g_pre, mod, w_router, b_router)
    counts = cnt[0, :N_EXPERTS].astype(I32)
    tile, grp, lo, hi, first, starts = _moe_visits(counts)
    experts = jnp.arange(N_EXPERTS, dtype=I32)[:, None, None]
    group_start = jnp.sum(jnp.where(ri[None, :TOP_K] == experts, starts.astype(I32)[:, None, None], 0), axis=0)
    dest = (group_start + ri[TOP_K:]).reshape(-1).astype(I32)
    xs = _dispatch(dest, h)
    ys = _grouped_mlp((tile, grp, lo, hi, first), xs, layer, w_up, b_up, w_down, b_down)
    return _combine(dest, ys, rw, x, mod, g_post, split_out)


def kernel(x_prompt, x_sample, cache_k, cache_v, state_fwd, state_bwd, c, c_ctx, w_ada, b_ada, g_pre_mix, g_post_mix, g_pre_ffn, g_post_ffn, na_w_qkv, na_rpb, na_w_out, gla_w_in, gla_w_g1_fwd, gla_w_g2_fwd, gla_b_g_fwd, gla_w_g1_bwd, gla_w_g2_bwd, gla_b_g_bwd, gla_norm_g, gla_w_out, moe_w_router, moe_b_router, moe_w_up, moe_b_up, moe_w_down, moe_b_down):
    x_ctx, x_lat = x_prompt.reshape(T_CTX, D), x_sample.reshape(T_LAT, D)
    cond = jnp.concatenate([c_ctx[None, :], c, jnp.zeros((8 - 1 - N_LAT_SEQ, D), F32)], axis=0)
    mods = _modulation(cond, w_ada, b_ada)

    qkv, new_k, new_v = _norm_mod_matmul((x_ctx, x_lat), g_pre_mix[0], mods[0], na_w_qkv[0], 0, kv_out=True)
    a_ctx = _ctx_attention(qkv)
    a_lat = _na_attention(qkv, cache_k[:, 0], cache_v[:, 0], na_rpb[0])
    x = _proj_residual(a_ctx, a_lat, x_ctx, x_lat, mods[0], g_post_mix[0], na_w_out[0], 2)
    new_k = new_k.reshape(N_CTX_SEQ, 1, CTX_LEN, NA_HEADS, NA_HD)
    new_v = new_v.reshape(N_CTX_SEQ, 1, CTX_LEN, NA_HEADS, NA_HD)
    x = _moe(x, mods[0], g_pre_ffn[0], g_post_ffn[0], moe_w_router[0], moe_b_router[0],
             0, moe_w_up, moe_b_up, moe_w_down, moe_b_down)

    hk = GLA_HEADS * GLA_DK
    gw1 = (jnp.zeros((D, LANES), F32).at[:, :GLA_RANK].set(gla_w_g1_fwd[0])
           .at[:, GLA_RANK:2 * GLA_RANK].set(gla_w_g1_bwd[0]))
    gw2 = (jnp.zeros((LANES, 2 * hk), F32).at[:GLA_RANK, :hk].set(gla_w_g2_fwd[0])
           .at[GLA_RANK:2 * GLA_RANK, hk:].set(gla_w_g2_bwd[0]))
    gb = jnp.concatenate([gla_b_g_fwd[0], gla_b_g_bwd[0]]).reshape(1, 2 * hk)
    proj, gates = _norm_mod_matmul(x, g_pre_mix[1], mods[1], gla_w_in[0], 0, gate_params=(gw1, gw2, gb))
    of_c, ob_c, s_f, s_b = _gla_scan(proj, gates, seq_len=CTX_LEN, n_seq=N_CTX_SEQ, seq_off=0, out_state=True)
    of_l, ob_l = _gla_scan(proj, gates, seq_len=LAT_LEN, n_seq=N_LAT_SEQ, seq_off=T_CTX // LAT_LEN,
                           s_f0=state_fwd[:, 0], s_b0=state_bwd[:, 0], rotary=True)
    x = _gla_output(of_c, ob_c, of_l, ob_l, proj, gla_norm_g[0], x, mods[1], g_post_mix[1], gla_w_out[0])
    x = _moe(x, mods[1], g_pre_ffn[1], g_post_ffn[1], moe_w_router[1], moe_b_router[1],
             1, moe_w_up, moe_b_up, moe_w_down, moe_b_down, split_out=True)

    return (x[0].reshape(N_CTX_SEQ, CTX_LEN, D), x[1].reshape(N_LAT_SEQ, LAT_LEN, D),
            new_k, new_v, s_f[:, None], s_b[:, None])
```
